```python
import jax, jax.numpy as jnp
from jax import lax
import numpy as np

D_MODEL = 1024
BATCH = 8
SEQ = 8192
DEPTH = 1
DEC_BATCH = 32
DEC_SEQ = 64
PAST_LEN = 4096

CHUNK = 64
N_HEADS = 16
HEAD_DIM = 64
D_ATTN = N_HEADS * HEAD_DIM
D_CONV = D_MODEL
CONV_K = 31
D_FF = 2816
FFN_K = 3
PLE_DIM = 256
Q_BLOCK = 128
LN_EPS = 1e-5
ALPHA = (2.0 * DEPTH) ** 0.25
BETA = (8.0 * DEPTH) ** -0.25

IN_SPLITS = (D_CONV, 2 * D_CONV, 2 * D_CONV + D_ATTN, 2 * D_CONV + 2 * D_ATTN,
             2 * D_CONV + 3 * D_ATTN, 2 * D_CONV + 3 * D_ATTN + N_HEADS,
             2 * D_CONV + 3 * D_ATTN + N_HEADS + D_MODEL)
N_IN = 2 * D_CONV + 3 * D_ATTN + N_HEADS + 2 * D_MODEL

kernel_name = 'streaming_conformer_fox_hybrid_step'


def layer_norm(x, g, b):
    xf = x.astype(jnp.float32)
    mu = jnp.mean(xf, axis=-1, keepdims=True)
    var = jnp.mean(jnp.square(xf - mu), axis=-1, keepdims=True)
    y = (xf - mu) * lax.rsqrt(var + LN_EPS)
    return (y * g.astype(jnp.float32) + b.astype(jnp.float32)).astype(x.dtype)


def causal_dwconv(hist, u, w, b):
    xp = jnp.concatenate([hist.astype(u.dtype), u], axis=1)
    c = u.shape[-1]
    y = lax.conv_general_dilated(xp, w.astype(u.dtype)[:, None, :], window_strides=(1,),
                                 padding='VALID', dimension_numbers=('NWC', 'WIO', 'NWC'),
                                 feature_group_count=c)
    return y + b.astype(u.dtype), xp[:, xp.shape[1] - hist.shape[1]:]


def fox_block(q, cq, q_pos, k, v, ck, k_pos):
    s = jnp.einsum('bqhd,bkhd->bhqk', q, k, preferred_element_type=jnp.float32) * (HEAD_DIM ** -0.5)
    decay = jnp.transpose(cq, (0, 2, 1))[..., :, None] - jnp.transpose(ck, (0, 2, 1))[..., None, :]
    mask = k_pos[None, :] <= q_pos[:, None]
    p = jax.nn.softmax(jnp.where(mask, s + decay, -jnp.inf), axis=-1)
    return jnp.einsum('bhqk,bkhd->bqhd', p.astype(v.dtype), v)


def fox_prompt(q, k, v, logf):
    b, s = q.shape[0], q.shape[1]
    c = jnp.cumsum(logf.astype(jnp.float32), axis=1)
    nb = s // Q_BLOCK
    qb = q.reshape(b, nb, Q_BLOCK, N_HEADS, HEAD_DIM).transpose(1, 0, 2, 3, 4)
    cb = c.reshape(b, nb, Q_BLOCK, N_HEADS).transpose(1, 0, 2, 3)
    starts = jnp.arange(nb, dtype=jnp.int32) * Q_BLOCK
    k_pos = jnp.arange(s, dtype=jnp.int32)

    def step(args):
        qi, ci, st = args
        return fox_block(qi, ci, st + jnp.arange(Q_BLOCK, dtype=jnp.int32), k, v, c, k_pos)

    o = lax.map(step, (qb, cb, starts))
    return o.transpose(1, 0, 2, 3, 4).reshape(b, s, N_HEADS, HEAD_DIM)


def fox_sample(q, k_new, v_new, logf_new, k_hist, v_hist, logf_hist):
    p_len, t = k_hist.shape[1], q.shape[1]
    k = jnp.concatenate([k_hist.astype(k_new.dtype), k_new], axis=1)
    v = jnp.concatenate([v_hist.astype(v_new.dtype), v_new], axis=1)
    c = jnp.cumsum(jnp.concatenate([logf_hist.astype(jnp.float32), logf_new.astype(jnp.float32)], axis=1), axis=1)
    q_pos = p_len + jnp.arange(t, dtype=jnp.int32)
    k_pos = jnp.arange(p_len + t, dtype=jnp.int32)
    return fox_block(q, c[:, p_len:], q_pos, k, v, c, k_pos)


def token_mixer(h, conv_hist, attend, w_in, b_f, conv_w, conv_b, conv_g, conv_beta,
                w_conv_out, w_attn_out, w_o):
    bsz, t = h.shape[0], h.shape[1]
    z = h @ w_in
    ga_, gg_, q, k, v, fl, g_conv, g_attn = jnp.split(z, IN_SPLITS, axis=-1)
    u = ga_ * jax.nn.sigmoid(gg_)
    uc, conv_new = causal_dwconv(conv_hist, u, conv_w, conv_b)
    conv_out = jax.nn.silu(layer_norm(uc, conv_g, conv_beta)) @ w_conv_out
    q = q.reshape(bsz, t, N_HEADS, HEAD_DIM)
    k = k.reshape(bsz, t, N_HEADS, HEAD_DIM)
    v = v.reshape(bsz, t, N_HEADS, HEAD_DIM)
    logf = jax.nn.log_sigmoid(fl.astype(jnp.float32) + b_f.astype(jnp.float32))
    o = attend(q, k, v, logf)
    attn_out = o.reshape(bsz, t, D_ATTN) @ w_attn_out
    mix = jax.nn.sigmoid(g_conv) * conv_out + jax.nn.sigmoid(g_attn) * attn_out
    return mix @ w_o, conv_new, k, v, logf


def conv_ffn(x, hist, w_up, dw_w, dw_b, w_down):
    a, b = jnp.split(x @ w_up, 2, axis=-1)
    ac, new_hist = causal_dwconv(hist, a, dw_w, dw_b)
    return (jax.nn.silu(ac) * b) @ w_down, new_hist


def trunk_layer(x, p, conv_hist, ffn_hist, attend, i, w_in, b_f, conv_dw_w, conv_dw_b, conv_ln_g,
                conv_ln_b, w_conv_out, w_attn_out, w_o, ln1_g, ln1_b, w_ffn_up, ffn_dw_w, ffn_dw_b,
                w_ffn_down, ln2_g, ln2_b, w_ple, w_ple_gate):
    mix, conv_new, k, v, logf = token_mixer(x, conv_hist, attend, w_in[i], b_f[i], conv_dw_w[i],
                                            conv_dw_b[i], conv_ln_g[i], conv_ln_b[i],
                                            w_conv_out[i], w_attn_out[i], w_o[i])
    x = layer_norm(ALPHA * x + mix, ln1_g[i], ln1_b[i])
    f, ffn_new = conv_ffn(x, ffn_hist, w_ffn_up[i], ffn_dw_w[i], ffn_dw_b[i], w_ffn_down[i])
    ple = jax.nn.sigmoid(x @ w_ple_gate[i]) * (p.astype(x.dtype) @ w_ple[i])
    x = layer_norm(ALPHA * x + f + ple, ln2_g[i], ln2_b[i])
    return x, k, v, logf, conv_new, ffn_new


def setup_inputs(seed: int = 0) -> dict:
    key = jax.random.key(seed)
    ks = jax.random.split(key, 32)

    def nrm(k, shape, s):
        return jax.random.normal(k, shape, jnp.float32) * s

    b_f = jnp.linspace(1.0, 6.0, N_HEADS, dtype=jnp.float32)[None, :] + nrm(ks[0], (DEPTH, N_HEADS), 0.1)
    return {
        'x_prompt': nrm(ks[1], (BATCH, SEQ, D_MODEL), 1.0),
        'x_sample': nrm(ks[2], (DEC_BATCH, DEC_SEQ, D_MODEL), 1.0),
        'cache_k': nrm(ks[3], (DEPTH, DEC_BATCH, PAST_LEN, N_HEADS, HEAD_DIM), 1.0),
        'cache_v': nrm(ks[4], (DEPTH, DEC_BATCH, PAST_LEN, N_HEADS, HEAD_DIM), 1.0),
        'cache_logf': jax.nn.log_sigmoid(b_f[:, None, None, :] + nrm(ks[5], (DEPTH, DEC_BATCH, PAST_LEN, N_HEADS), 1.0)),
        'state_conv': nrm(ks[6], (DEPTH, DEC_BATCH, CONV_K - 1, D_CONV), 0.5),
        'state_ffn_conv': nrm(ks[7], (DEPTH, DEC_BATCH, FFN_K - 1, D_FF), 1.0),
        'p_prompt': nrm(ks[8], (DEPTH, BATCH, SEQ, PLE_DIM), 1.0),
        'p_sample': nrm(ks[9], (DEPTH, DEC_BATCH, DEC_SEQ, PLE_DIM), 1.0),
        'ln0_g': 1.0 + nrm(ks[10], (D_MODEL,), 0.02),
        'ln0_b': nrm(ks[11], (D_MODEL,), 0.02),
        'w_in': nrm(ks[12], (DEPTH, D_MODEL, N_IN), D_MODEL ** -0.5),
        'b_f': b_f,
        'conv_dw_w': nrm(ks[13], (DEPTH, CONV_K, D_CONV), CONV_K ** -0.5),
        'conv_dw_b': nrm(ks[14], (DEPTH, D_CONV), 0.02),
        'conv_ln_g': 1.0 + nrm(ks[15], (DEPTH, D_CONV), 0.02),
        'conv_ln_b': nrm(ks[16], (DEPTH, D_CONV), 0.02),
        'w_conv_out': nrm(ks[17], (DEPTH, D_CONV, D_MODEL), BETA * D_CONV ** -0.5),
        'w_attn_out': nrm(ks[18], (DEPTH, D_ATTN, D_MODEL), BETA * D_ATTN ** -0.5),
        'w_o': nrm(ks[19], (DEPTH, D_MODEL, D_MODEL), BETA * D_MODEL ** -0.5),
        'ln1_g': 1.0 + nrm(ks[20], (DEPTH, D_MODEL), 0.02),
        'ln1_b': nrm(ks[21], (DEPTH, D_MODEL), 0.02),
        'w_ffn_up': nrm(ks[22], (DEPTH, D_MODEL, 2 * D_FF), D_MODEL ** -0.5),
        'ffn_dw_w': nrm(ks[23], (DEPTH, FFN_K, D_FF), FFN_K ** -0.5),
        'ffn_dw_b': nrm(ks[24], (DEPTH, D_FF), 0.02),
        'w_ffn_down': nrm(ks[25], (DEPTH, D_FF, D_MODEL), BETA * D_FF ** -0.5),
        'ln2_g': 1.0 + nrm(ks[26], (DEPTH, D_MODEL), 0.02),
        'ln2_b': nrm(ks[27], (DEPTH, D_MODEL), 0.02),
        'w_ple': nrm(ks[28], (DEPTH, PLE_DIM, D_MODEL), PLE_DIM ** -0.5),
        'w_ple_gate': nrm(ks[29], (DEPTH, D_MODEL, D_MODEL), D_MODEL ** -0.5),
    }


def reference(x_prompt, x_sample, cache_k, cache_v, cache_logf, state_conv, state_ffn_conv,
              p_prompt, p_sample, ln0_g, ln0_b, w_in, b_f, conv_dw_w, conv_dw_b, conv_ln_g,
              conv_ln_b, w_conv_out, w_attn_out, w_o, ln1_g, ln1_b, w_ffn_up, ffn_dw_w, ffn_dw_b,
              w_ffn_down, ln2_g, ln2_b, w_ple, w_ple_gate):
    weights = (w_in, b_f, conv_dw_w, conv_dw_b, conv_ln_g, conv_ln_b, w_conv_out, w_attn_out, w_o,
               ln1_g, ln1_b, w_ffn_up, ffn_dw_w, ffn_dw_b, w_ffn_down, ln2_g, ln2_b, w_ple, w_ple_gate)

    xp = layer_norm(x_prompt, ln0_g, ln0_b)
    bp = x_prompt.shape[0]
    kp, vp, lp, cp, fp = [], [], [], [], []
    for i in range(DEPTH):
        conv_h0 = jnp.zeros((bp, CONV_K - 1, D_CONV), xp.dtype)
        ffn_h0 = jnp.zeros((bp, FFN_K - 1, D_FF), xp.dtype)
        xp, k_i, v_i, l_i, c_i, f_i = trunk_layer(xp, p_prompt[i], conv_h0, ffn_h0, fox_prompt, i, *weights)
        kp.append(k_i); vp.append(v_i); lp.append(l_i); cp.append(c_i); fp.append(f_i)

    xs = layer_norm(x_sample, ln0_g, ln0_b)
    ks_, vs_, ls_, cs_, fs_ = [], [], [], [], []
    for i in range(DEPTH):
        def attend(q, k, v, logf, i=i):
            return fox_sample(q, k, v, logf, cache_k[i], cache_v[i], cache_logf[i])
        xs, k_i, v_i, l_i, c_i, f_i = trunk_layer(xs, p_sample[i], state_conv[i], state_ffn_conv[i], attend, i, *weights)
        ks_.append(k_i); vs_.append(v_i); ls_.append(l_i); cs_.append(c_i); fs_.append(f_i)

    return (xp, xs,
            jnp.stack(kp), jnp.stack(vp), jnp.stack(lp), jnp.stack(cp), jnp.stack(fp),
            jnp.stack(ks_), jnp.stack(vs_), jnp.stack(ls_), jnp.stack(cs_), jnp.stack(fs_))
```

```python
import functools

import jax
import jax.numpy as jnp
from jax import lax
from jax.experimental import pallas as pl
from jax.experimental.pallas import tpu as pltpu

N_HEADS = 16
HEAD_DIM = 64
CONV_K = 31
FFN_K = 3
LN_EPS = 1e-5

LANES = 128
SUBLANES = 8
HEADS_PER_TILE = LANES // HEAD_DIM
CONV_HALO = 32
FFN_HALO = SUBLANES
MASK_VALUE = -1e30
VMEM_LIMIT = 56 * 1024 * 1024

F32 = jnp.float32
BF16 = jnp.bfloat16


def _layer_norm(x, g, b):
    mu = jnp.mean(x, axis=-1, keepdims=True)
    xc = x - mu
    var = jnp.mean(xc * xc, axis=-1, keepdims=True)
    return xc * lax.rsqrt(var + LN_EPS) * g + b


def _resident(shape):
    return pl.BlockSpec(shape, lambda *_: (0,) * len(shape), pipeline_mode=pl.Buffered(1))


def _params(semantics):
    return pltpu.CompilerParams(dimension_semantics=semantics, vmem_limit_bytes=VMEM_LIMIT)


def _inproj_kernel(x_ref, g_ref, b_ref, wa_ref, wg_ref, wq_ref, wk_ref, wv_ref, wf_ref, wgc_ref,
                   wga_ref, bf_ref, u_ref, q_ref, k_ref, v_ref, lf_ref, gc_ref, ga_ref):
    xn = _layer_norm(x_ref[...], g_ref[...], b_ref[...]).astype(BF16)

    def proj(w_ref):
        return jnp.dot(xn, w_ref[...], preferred_element_type=F32)

    u_ref[...] = proj(wa_ref) * jax.nn.sigmoid(proj(wg_ref))
    q_ref[...] = (proj(wq_ref) * (HEAD_DIM ** -0.5)).astype(BF16)
    k_ref[...] = proj(wk_ref)
    v_ref[...] = proj(wv_ref)
    lf_ref[...] = jax.nn.log_sigmoid(proj(wf_ref) + bf_ref[...])
    gc_ref[...] = jax.nn.sigmoid(proj(wgc_ref)).astype(BF16)
    ga_ref[...] = jax.nn.sigmoid(proj(wga_ref)).astype(BF16)


def _inproj(x, ln_g, ln_b, w, b_f, tm):
    n, d = x.shape
    row = lambda width: pl.BlockSpec((tm, width), lambda i: (i, 0))
    out_shape = (jax.ShapeDtypeStruct((n, d), F32),
                 jax.ShapeDtypeStruct((n, d), BF16),
                 jax.ShapeDtypeStruct((n, d), F32),
                 jax.ShapeDtypeStruct((n, d), F32),
                 jax.ShapeDtypeStruct((n, N_HEADS), F32),
                 jax.ShapeDtypeStruct((n, d), BF16),
                 jax.ShapeDtypeStruct((n, d), BF16))
    return pl.pallas_call(
        _inproj_kernel,
        grid=(n // tm,),
        in_specs=[row(d), _resident((1, d)), _resident((1, d))]
                 + [_resident(w[name].shape) for name in ("a", "g", "q", "k", "v", "f", "gc", "ga")]
                 + [_resident((1, N_HEADS))],
        out_specs=(row(d), row(d), row(d), row(d), row(N_HEADS), row(d), row(d)),
        out_shape=out_shape,
        compiler_params=_params(("arbitrary",)),
        name="inproj",
    )(x, ln_g, ln_b, w["a"], w["g"], w["q"], w["k"], w["v"], w["f"], w["gc"], w["ga"], b_f)


def _cumsum_kernel(x_ref, o_ref):
    x = x_ref[0]
    nblk = x.shape[1]
    lane = lax.broadcasted_iota(jnp.int32, x.shape, 2)
    shift = 1
    while shift < LANES:
        x = x + jnp.where(lane >= shift, pltpu.roll(x, shift, 2), 0.0)
        shift *= 2
    tot = jnp.broadcast_to(x[:, :, LANES - 1:LANES], x.shape)
    blk = lax.broadcasted_iota(jnp.int32, x.shape, 1)
    inc = tot
    shift = 1
    while shift < nblk:
        inc = inc + jnp.where(blk >= shift, pltpu.roll(inc, shift, 1), 0.0)
        shift *= 2
    o_ref[0] = x + (inc - tot)


def _cumsum_time(logf_t):
    b, h, l = logf_t.shape
    nblk = pl.cdiv(l, LANES)
    x = jnp.pad(logf_t, ((0, 0), (0, 0), (0, nblk * LANES - l))).reshape(b, h, nblk, LANES)
    spec = pl.BlockSpec((1, h, nblk, LANES), lambda i: (i, 0, 0, 0))
    out = pl.pallas_call(
        _cumsum_kernel, grid=(b,), in_specs=[spec], out_specs=spec,
        out_shape=jax.ShapeDtypeStruct(x.shape, F32),
        compiler_params=_params(("arbitrary",)), name="cumsum",
    )(x)
    return out.reshape(b, h, nblk * LANES)[:, :, :l]


def _convbranch_kernel(u_ref, hist_ref, w_ref, b_ref, g_ref, beta_ref, wo_ref, o_ref, xp_ref, uc_ref,
                       *, tt, rows):
    t = pl.program_id(1)

    @pl.when(t == 0)
    def _():
        xp_ref[0:CONV_HALO, :] = hist_ref[0]

    @pl.when(t > 0)
    def _():
        xp_ref[0:CONV_HALO, :] = xp_ref[tt:tt + CONV_HALO, :]

    xp_ref[CONV_HALO:CONV_HALO + tt, :] = u_ref[0]
    first = CONV_HALO - (CONV_K - 1)
    d = u_ref.shape[-1]

    def lane_block(c, carry):
        c0 = pl.multiple_of(c * LANES, LANES)
        cols = pl.ds(c0, LANES)
        for r0 in range(0, tt, rows):
            acc = jnp.broadcast_to(b_ref[:, cols], (rows, LANES))
            for k in range(CONV_K):
                acc = acc + w_ref[k:k + 1, cols] * xp_ref[r0 + first + k:r0 + first + k + rows, cols]
            uc_ref[r0:r0 + rows, cols] = acc
        return carry

    lax.fori_loop(0, d // LANES, lane_block, 0)
    act = jax.nn.silu(_layer_norm(uc_ref[...], g_ref[...], beta_ref[...])).astype(BF16)
    o_ref[0] = jnp.dot(act, wo_ref[...], preferred_element_type=F32).astype(BF16)


def _convbranch(u, hist, dw_w, dw_b, ln_g, ln_b, w_out, tt):
    b, t, d = u.shape
    hist_pad = jnp.pad(hist.astype(F32), ((0, 0), (CONV_HALO - (CONV_K - 1), 0), (0, 0)))
    tile = pl.BlockSpec((1, tt, d), lambda i, j: (i, j, 0))
    return pl.pallas_call(
        functools.partial(_convbranch_kernel, tt=tt, rows=min(tt, 64)),
        grid=(b, t // tt),
        in_specs=[tile, pl.BlockSpec((1, CONV_HALO, d), lambda i, j: (i, 0, 0)),
                  _resident((CONV_K, d)), _resident((1, d)), _resident((1, d)), _resident((1, d)),
                  _resident((d, d))],
        out_specs=tile,
        out_shape=jax.ShapeDtypeStruct((b, t, d), BF16),
        scratch_shapes=[pltpu.VMEM((CONV_HALO + tt, d), F32), pltpu.VMEM((tt, d), F32)],
        compiler_params=_params(("arbitrary", "arbitrary")),
        name="convbranch",
    )(u, hist_pad, dw_w, dw_b, ln_g, ln_b, w_out)


def _head_masks(q):
    lane = lax.broadcasted_iota(jnp.int32, q.shape, 1)
    return [jnp.where((lane >= h * HEAD_DIM) & (lane < (h + 1) * HEAD_DIM), q, jnp.zeros_like(q))
            for h in range(HEADS_PER_TILE)]


def _softmax_step(qh, k, v, bias, m_ref, l_ref, acc_ref, mask):
    s = lax.dot_general(qh, k, (((1,), (1,)), ((), ())), preferred_element_type=F32) + bias
    if mask is not None:
        s = jnp.where(mask, s, MASK_VALUE)
    m_old = m_ref[...]
    m_new = jnp.maximum(m_old, jnp.max(s, axis=1, keepdims=True))
    alpha = jnp.exp(m_old - m_new)
    p = jnp.exp(s - m_new)
    l_ref[...] = alpha * l_ref[...] + jnp.sum(p, axis=1, keepdims=True)
    acc_ref[...] = alpha * acc_ref[...] + jnp.dot(p.astype(BF16), v, preferred_element_type=F32)
    m_ref[...] = m_new


def _merge_heads(acc_refs, l_refs):
    lane = lax.broadcasted_iota(jnp.int32, acc_refs[0].shape, 1)
    out = acc_refs[0][...] / l_refs[0][...]
    for h in range(1, HEADS_PER_TILE):
        out = jnp.where(lane >= h * HEAD_DIM, acc_refs[h][...] / l_refs[h][...], out)
    return out


def _attn_prompt_kernel(q_ref, k_ref, v_ref, nc_ref, o_ref, m_ref, l_ref, acc_ref, *, tq, tk):
    i = pl.program_id(2)
    qh = _head_masks(q_ref[0])
    m_ref[...] = jnp.full(m_ref.shape, MASK_VALUE, F32)
    l_ref[...] = jnp.zeros(l_ref.shape, F32)
    acc_ref[...] = jnp.zeros(acc_ref.shape, F32)

    def step(j, mask):
        k0 = pl.multiple_of(j * tk, tk)
        k = k_ref[0, pl.ds(k0, tk), :]
        v = v_ref[0, pl.ds(k0, tk), :]
        nc = nc_ref[0, 0, :, pl.ds(k0, tk)]
        for h in range(HEADS_PER_TILE):
            _softmax_step(qh[h], k, v, nc[h:h + 1, :], m_ref.at[h], l_ref.at[h], acc_ref.at[h], mask)

    n_full = (i * tq) // tk

    def full_step(j, carry):
        step(j, None)
        return carry

    lax.fori_loop(0, n_full, full_step, 0)
    row = lax.broadcasted_iota(jnp.int32, (tq, tk), 0)
    col = lax.broadcasted_iota(jnp.int32, (tq, tk), 1)
    for jd in range(tq // tk):
        step(n_full + jd, col + jd * tk <= row)
    o_ref[0] = _merge_heads([acc_ref.at[h] for h in range(HEADS_PER_TILE)],
                            [l_ref.at[h] for h in range(HEADS_PER_TILE)]).astype(BF16)


def _attn_prompt(q, k, v, neg_c, tq, tk):
    b, t, d = q.shape
    groups = d // LANES
    return pl.pallas_call(
        functools.partial(_attn_prompt_kernel, tq=tq, tk=tk),
        grid=(b, groups, t // tq),
        in_specs=[pl.BlockSpec((1, tq, LANES), lambda bi, g, i: (bi, i, g)),
                  pl.BlockSpec((1, t, LANES), lambda bi, g, i: (bi, 0, g)),
                  pl.BlockSpec((1, t, LANES), lambda bi, g, i: (bi, 0, g)),
                  pl.BlockSpec((1, 1, HEADS_PER_TILE, t), lambda bi, g, i: (bi, g, 0, 0))],
        out_specs=pl.BlockSpec((1, tq, LANES), lambda bi, g, i: (bi, i, g)),
        out_shape=jax.ShapeDtypeStruct((b, t, d), BF16),
        scratch_shapes=[pltpu.VMEM((HEADS_PER_TILE, tq, 1), F32), pltpu.VMEM((HEADS_PER_TILE, tq, 1), F32),
                        pltpu.VMEM((HEADS_PER_TILE, tq, LANES), F32)],
        compiler_params=_params(("arbitrary", "arbitrary", "arbitrary")),
        name="attn_prompt",
    )(q, k, v, neg_c)


def _attn_sample_kernel(q_ref, kh_ref, vh_ref, nch_ref, kn_ref, vn_ref, ncn_ref, o_ref, m_ref, l_ref, acc_ref):
    j = pl.program_id(1)
    groups = q_ref.shape[-1] // LANES

    @pl.when(j == 0)
    def _():
        m_ref[...] = jnp.full(m_ref.shape, MASK_VALUE, F32)
        l_ref[...] = jnp.zeros(l_ref.shape, F32)
        acc_ref[...] = jnp.zeros(acc_ref.shape, F32)

    def update(k_ref_, v_ref_, nc_ref_, mask):
        for g in range(groups):
            cols = slice(g * LANES, (g + 1) * LANES)
            qh = _head_masks(q_ref[0, :, cols])
            k = k_ref_[0, :, cols].astype(BF16)
            v = v_ref_[0, :, cols].astype(BF16)
            for h in range(HEADS_PER_TILE):
                hh = g * HEADS_PER_TILE + h
                _softmax_step(qh[h], k, v, nc_ref_[0, hh:hh + 1, :], m_ref.at[hh], l_ref.at[hh], acc_ref.at[hh], mask)

    update(kh_ref, vh_ref, nch_ref, None)

    @pl.when(j == pl.num_programs(1) - 1)
    def _():
        t = q_ref.shape[1]
        row = lax.broadcasted_iota(jnp.int32, (t, t), 0)
        col = lax.broadcasted_iota(jnp.int32, (t, t), 1)
        update(kn_ref, vn_ref, ncn_ref, col <= row)
        for g in range(groups):
            hs = range(g * HEADS_PER_TILE, (g + 1) * HEADS_PER_TILE)
            o_ref[0, :, g * LANES:(g + 1) * LANES] = _merge_heads(
                [acc_ref.at[h] for h in hs], [l_ref.at[h] for h in hs]).astype(BF16)


def _attn_sample(q, k_new, v_new, neg_c_new, k_hist, v_hist, neg_c_hist, tk):
    b, t, d = q.shape
    p = k_hist.shape[1]
    new = pl.BlockSpec((1, t, d), lambda bi, j: (bi, 0, 0))
    hist = pl.BlockSpec((1, tk, d), lambda bi, j: (bi, j, 0))
    return pl.pallas_call(
        _attn_sample_kernel,
        grid=(b, p // tk),
        in_specs=[new, hist, hist, pl.BlockSpec((1, N_HEADS, tk), lambda bi, j: (bi, 0, j)),
                  new, new, pl.BlockSpec((1, N_HEADS, t), lambda bi, j: (bi, 0, 0))],
        out_specs=new,
        out_shape=jax.ShapeDtypeStruct((b, t, d), BF16),
        scratch_shapes=[pltpu.VMEM((N_HEADS, t, 1), F32), pltpu.VMEM((N_HEADS, t, 1), F32),
                        pltpu.VMEM((N_HEADS, t, LANES), F32)],
        compiler_params=_params(("arbitrary", "arbitrary")),
        name="attn_sample",
    )(q, k_hist, v_hist, neg_c_hist, k_new, v_new, neg_c_new)


def _merge_kernel(x_ref, g0_ref, b0_ref, cv_ref, o_ref, gc_ref, ga_ref, wao_ref, wo_ref, g1_ref, b1_ref,
                  y_ref, *, alpha):
    xn = _layer_norm(x_ref[...], g0_ref[...], b0_ref[...])
    attn_out = jnp.dot(o_ref[...], wao_ref[...], preferred_element_type=F32)
    mix = gc_ref[...].astype(F32) * cv_ref[...].astype(F32) + ga_ref[...].astype(F32) * attn_out
    y = jnp.dot(mix.astype(BF16), wo_ref[...], preferred_element_type=F32)
    y_ref[...] = _layer_norm(alpha * xn + y, g1_ref[...], b1_ref[...])


def _merge(x, ln0_g, ln0_b, conv_out, attn, gate_c, gate_a, w_attn_out, w_o, ln1_g, ln1_b, alpha, tm):
    n, d = x.shape
    row = pl.BlockSpec((tm, d), lambda i: (i, 0))
    vec = _resident((1, d))
    return pl.pallas_call(
        functools.partial(_merge_kernel, alpha=alpha),
        grid=(n // tm,),
        in_specs=[row, vec, vec, row, row, row, row, _resident((d, d)), _resident((d, d)), vec, vec],
        out_specs=row,
        out_shape=jax.ShapeDtypeStruct((n, d), F32),
        compiler_params=_params(("arbitrary",)),
        name="merge",
    )(x, ln0_g, ln0_b, conv_out, attn, gate_c, gate_a, w_attn_out, w_o, ln1_g, ln1_b)


def _ffn_kernel(x_ref, p_ref, hist_ref, wua_ref, wub_ref, dw_ref, db_ref, wd_ref, wpg_ref, wp_ref, g_ref, b_ref,
                y_ref, tail_ref, ap_ref, *, tt, alpha):
    t = pl.program_id(1)

    @pl.when(t == 0)
    def _():
        ap_ref[0:FFN_HALO, :] = hist_ref[0]

    @pl.when(t > 0)
    def _():
        ap_ref[0:FFN_HALO, :] = ap_ref[tt:tt + FFN_HALO, :]

    x = x_ref[0]
    xb = x.astype(BF16)
    ap_ref[FFN_HALO:FFN_HALO + tt, :] = jnp.dot(xb, wua_ref[...], preferred_element_type=F32)
    tail_ref[0] = ap_ref[tt:tt + FFN_HALO, :]
    first = FFN_HALO - (FFN_K - 1)
    ac = db_ref[...]
    for k in range(FFN_K):
        ac = ac + dw_ref[k:k + 1, :] * ap_ref[first + k:first + k + tt, :]
    gate = jnp.dot(xb, wub_ref[...], preferred_element_type=F32)
    f = jnp.dot((jax.nn.silu(ac) * gate).astype(BF16), wd_ref[...], preferred_element_type=F32)
    ple = (jax.nn.sigmoid(jnp.dot(xb, wpg_ref[...], preferred_element_type=F32))
           * jnp.dot(p_ref[0].astype(BF16), wp_ref[...], preferred_element_type=F32))
    y_ref[0] = _layer_norm(alpha * x + f + ple, g_ref[...], b_ref[...])


def _ffn(x, p, hist, w_up_a, w_up_b, dw_w, dw_b, w_down, w_pg, w_ple, ln_g, ln_b, alpha, tt):
    b, t, d = x.shape
    dff = w_up_a.shape[1]
    dp = p.shape[-1]
    hist_pad = jnp.pad(hist.astype(F32), ((0, 0), (FFN_HALO - (FFN_K - 1), 0), (0, 0)))
    tile = pl.BlockSpec((1, tt, d), lambda i, j: (i, j, 0))
    halo = pl.BlockSpec((1, FFN_HALO, dff), lambda i, j: (i, 0, 0))
    return pl.pallas_call(
        functools.partial(_ffn_kernel, tt=tt, alpha=alpha),
        grid=(b, t // tt),
        in_specs=[tile, pl.BlockSpec((1, tt, dp), lambda i, j: (i, j, 0)), halo,
                  _resident((d, dff)), _resident((d, dff)), _resident((FFN_K, dff)), _resident((1, dff)),
                  _resident((dff, d)), _resident((d, d)), _resident((dp, d)), _resident((1, d)), _resident((1, d))],
        out_specs=(tile, halo),
        out_shape=(jax.ShapeDtypeStruct((b, t, d), F32), jax.ShapeDtypeStruct((b, FFN_HALO, dff), F32)),
        scratch_shapes=[pltpu.VMEM((FFN_HALO + tt, dff), F32)],
        compiler_params=_params(("arbitrary", "arbitrary")),
        name="ffn",
    )(x, p, hist_pad, w_up_a, w_up_b, dw_w, dw_b, w_down, w_pg, w_ple, ln_g, ln_b)


def _tile(n, target):
    if n <= target:
        return n
    for cand in range(target, SUBLANES - 1, -SUBLANES):
        if n % cand == 0:
            return cand
    return n


def _layer(x, p, conv_hist, ffn_hist, cache, w, alpha):
    b, t, d = x.shape
    n = b * t
    assert t >= CONV_K - 1 and t % SUBLANES == 0
    tm = _tile(n, 512)
    tt = _tile(t, 256)
    u, q, k, v, logf, gate_c, gate_a = _inproj(x.reshape(n, d), w["ln_in_g"], w["ln_in_b"], w["in"], w["b_f"], tm)
    u = u.reshape(b, t, d)
    logf = logf.reshape(b, t, N_HEADS)
    conv_out = _convbranch(u, conv_hist, w["conv_dw_w"], w["conv_dw_b"], w["conv_ln_g"], w["conv_ln_b"],
                           w["conv_out"], tt)
    q3, k3, v3 = q.reshape(b, t, d), k.astype(BF16).reshape(b, t, d), v.astype(BF16).reshape(b, t, d)
    logf_t = jnp.swapaxes(logf, 1, 2)
    if cache is None:
        neg_c = -_cumsum_time(logf_t)
        tq = _tile(t, 512)
        attn = _attn_prompt(q3, k3, v3, neg_c.reshape(b, d // LANES, HEADS_PER_TILE, t), tq, tq)
    else:
        k_hist, v_hist, logf_hist = cache
        past = k_hist.shape[1]
        neg_c = -_cumsum_time(jnp.concatenate([jnp.swapaxes(logf_hist.astype(F32), 1, 2), logf_t], axis=2))
        attn = _attn_sample(q3, k3, v3, neg_c[:, :, past:], k_hist.reshape(b, past, d), v_hist.reshape(b, past, d),
                            neg_c[:, :, :past], _tile(past, 512))
    x1 = _merge(x.reshape(n, d), w["ln_in_g"], w["ln_in_b"], conv_out.reshape(n, d), attn.reshape(n, d),
                gate_c, gate_a, w["attn_out"], w["o"], w["ln1_g"], w["ln1_b"], alpha, tm)
    y, tail = _ffn(x1.reshape(b, t, d), p, ffn_hist, w["ffn_up_a"], w["ffn_up_b"], w["ffn_dw_w"], w["ffn_dw_b"],
                   w["ffn_down"], w["ple_gate"], w["ple"], w["ln2_g"], w["ln2_b"], alpha, tt)
    conv_new = u[:, t - (CONV_K - 1):]
    ffn_new = tail[:, FFN_HALO - (FFN_K - 1):]
    return (y, k.reshape(b, t, N_HEADS, HEAD_DIM), v.reshape(b, t, N_HEADS, HEAD_DIM), logf, conv_new, ffn_new)


def kernel(x_prompt, x_sample, cache_k, cache_v, cache_logf, state_conv, state_ffn_conv, p_prompt, p_sample, ln0_g, ln0_b, w_in, b_f, conv_dw_w, conv_dw_b, conv_ln_g, conv_ln_b, w_conv_out, w_attn_out, w_o, ln1_g, ln1_b, w_ffn_up, ffn_dw_w, ffn_dw_b, w_ffn_down, ln2_g, ln2_b, w_ple, w_ple_gate):
    depth = w_in.shape[0]
    assert depth == 1, "LN0 is fused into the first layer's projections; deeper stacks need an identity LN for later layers"
    alpha = (2.0 * depth) ** 0.25
    d = x_prompt.shape[-1]
    d_attn = N_HEADS * HEAD_DIM
    d_ff = w_ffn_up.shape[-1] // 2
    row = lambda a: a.reshape(1, -1).astype(F32)
    i = 0
    wi = w_in[i].astype(BF16)
    bounds = [0, d, 2 * d, 2 * d + d_attn, 2 * d + 2 * d_attn, 2 * d + 3 * d_attn, 2 * d + 3 * d_attn + N_HEADS,
              2 * d + 3 * d_attn + N_HEADS + d, 2 * d + 3 * d_attn + N_HEADS + 2 * d]
    names = ("a", "g", "q", "k", "v", "f", "gc", "ga")
    wup = w_ffn_up[i].astype(BF16)
    w = {
        "ln_in_g": row(ln0_g), "ln_in_b": row(ln0_b),
        "in": {nm: wi[:, bounds[j]:bounds[j + 1]] for j, nm in enumerate(names)},
        "b_f": row(b_f[i]),
        "conv_dw_w": conv_dw_w[i].astype(F32), "conv_dw_b": row(conv_dw_b[i]),
        "conv_ln_g": row(conv_ln_g[i]), "conv_ln_b": row(conv_ln_b[i]),
        "conv_out": w_conv_out[i].astype(BF16), "attn_out": w_attn_out[i].astype(BF16), "o": w_o[i].astype(BF16),
        "ln1_g": row(ln1_g[i]), "ln1_b": row(ln1_b[i]),
        "ffn_up_a": wup[:, :d_ff], "ffn_up_b": wup[:, d_ff:],
        "ffn_dw_w": ffn_dw_w[i].astype(F32), "ffn_dw_b": row(ffn_dw_b[i]),
        "ffn_down": w_ffn_down[i].astype(BF16), "ple_gate": w_ple_gate[i].astype(BF16), "ple": w_ple[i].astype(BF16),
        "ln2_g": row(ln2_g[i]), "ln2_b": row(ln2_b[i]),
    }
    bp = x_prompt.shape[0]
    zeros_conv = jnp.zeros((bp, CONV_K - 1, d), F32)
    zeros_ffn = jnp.zeros((bp, FFN_K - 1, d_ff), F32)
    yp, kp, vp, lp, cp, fp = _layer(x_prompt, p_prompt[i], zeros_conv, zeros_ffn, None, w, alpha)
    ys, ks, vs, ls, cs, fs = _layer(x_sample, p_sample[i], state_conv[i], state_ffn_conv[i],
                                    (cache_k[i], cache_v[i], cache_logf[i]), w, alpha)
    stack = lambda a: a[None]
    return (yp, ys, stack(kp), stack(vp), stack(lp), stack(cp), stack(fp),
            stack(ks), stack(vs), stack(ls), stack(cs), stack(fs))
```

```python
import functools

import jax
import jax.numpy as jnp
from jax import lax
from jax.experimental import pallas as pl
from jax.experimental.pallas import tpu as pltpu

N_HEADS = 16
HEAD_DIM = 64
CONV_K = 31
FFN_K = 3
LN_EPS = 1e-5

LANES = 128
SUBLANES = 8
HEADS_PER_TILE = LANES // HEAD_DIM
CONV_HALO = 32
FFN_HALO = SUBLANES
MASK_VALUE = -1e30
LOG2E = 1.4426950408889634
Q_SCALE = LOG2E * HEAD_DIM ** -0.5
BIAS_PIECES = 3
ONES_ROWS = 16
VMEM_LIMIT = 56 * 1024 * 1024

F32 = jnp.float32
BF16 = jnp.bfloat16


def _layer_norm(x, g, b):
    mu = jnp.mean(x, axis=-1, keepdims=True)
    xc = x - mu
    var = jnp.mean(xc * xc, axis=-1, keepdims=True)
    return xc * lax.rsqrt(var + LN_EPS) * g + b


def _resident(shape):
    return pl.BlockSpec(shape, lambda *_: (0,) * len(shape), pipeline_mode=pl.Buffered(1))


def _params(semantics):
    return pltpu.CompilerParams(dimension_semantics=semantics, vmem_limit_bytes=VMEM_LIMIT)


def _inproj_kernel(x_ref, g_ref, b_ref, wa_ref, wg_ref, wq_ref, wk_ref, wv_ref, wf_ref, wgc_ref,
                   wga_ref, bf_ref, u_ref, q_ref, k_ref, v_ref, kb_ref, vb_ref, lf_ref, gc_ref, ga_ref,
                   *, transpose_v):
    xn = _layer_norm(x_ref[...], g_ref[...], b_ref[...]).astype(BF16)

    def proj(w_ref):
        return jnp.dot(xn, w_ref[...], preferred_element_type=F32)

    u_ref[...] = proj(wa_ref) * jax.nn.sigmoid(proj(wg_ref))
    q_ref[...] = (proj(wq_ref) * Q_SCALE).astype(BF16)
    k = proj(wk_ref)
    k_ref[...] = k
    kb_ref[...] = k.astype(BF16)
    v = proj(wv_ref)
    v_ref[...] = v
    if transpose_v:
        vb_ref[0] = jnp.transpose(v).astype(BF16)
    else:
        vb_ref[...] = v.astype(BF16)
    lf_ref[...] = jax.nn.log_sigmoid(proj(wf_ref) + bf_ref[...])
    gc_ref[...] = jax.nn.sigmoid(proj(wgc_ref)).astype(BF16)
    ga_ref[...] = jax.nn.sigmoid(proj(wga_ref)).astype(BF16)


def _inproj(x, ln_g, ln_b, w, b_f, tm, t, transpose_v):
    n, d = x.shape
    tiles_per_stream = t // tm
    row = lambda width: pl.BlockSpec((tm, width), lambda i: (i, 0))
    if transpose_v:
        vb_shape = jax.ShapeDtypeStruct((n // t, d, t), BF16)
        vb_spec = pl.BlockSpec((1, d, tm), lambda i: (i // tiles_per_stream, 0, i % tiles_per_stream))
    else:
        vb_shape, vb_spec = jax.ShapeDtypeStruct((n, d), BF16), row(d)
    out_shape = (jax.ShapeDtypeStruct((n, d), F32), jax.ShapeDtypeStruct((n, d), BF16),
                 jax.ShapeDtypeStruct((n, d), F32), jax.ShapeDtypeStruct((n, d), F32),
                 jax.ShapeDtypeStruct((n, d), BF16), vb_shape,
                 jax.ShapeDtypeStruct((n, N_HEADS), F32),
                 jax.ShapeDtypeStruct((n, d), BF16), jax.ShapeDtypeStruct((n, d), BF16))
    return pl.pallas_call(
        functools.partial(_inproj_kernel, transpose_v=transpose_v),
        grid=(n // tm,),
        in_specs=[row(d), _resident((1, d)), _resident((1, d))]
                 + [_resident(w[name].shape) for name in ("a", "g", "q", "k", "v", "f", "gc", "ga")]
                 + [_resident((1, N_HEADS))],
        out_specs=(row(d), row(d), row(d), row(d), row(d), vb_spec, row(N_HEADS), row(d), row(d)),
        out_shape=out_shape,
        compiler_params=_params(("arbitrary",)),
        name="inproj",
    )(x, ln_g, ln_b, w["a"], w["g"], w["q"], w["k"], w["v"], w["f"], w["gc"], w["ga"], b_f)


def _cumsum_kernel(x_ref, o_ref, p_ref):
    x = x_ref[0]
    nblk = x.shape[1]
    lane = lax.broadcasted_iota(jnp.int32, x.shape, 2)
    shift = 1
    while shift < LANES:
        x = x + jnp.where(lane >= shift, pltpu.roll(x, shift, 2), 0.0)
        shift *= 2
    tot = jnp.broadcast_to(x[:, :, LANES - 1:LANES], x.shape)
    blk = lax.broadcasted_iota(jnp.int32, x.shape, 1)
    inc = tot
    shift = 1
    while shift < nblk:
        inc = inc + jnp.where(blk >= shift, pltpu.roll(inc, shift, 1), 0.0)
        shift *= 2
    bias = (x + (inc - tot)) * (-LOG2E)
    o_ref[0] = bias
    rest = bias
    for piece in range(BIAS_PIECES):
        part = rest.astype(BF16)
        p_ref[0, piece] = part
        rest = rest - part.astype(F32)


def _decay_bias(logf_t):
    b, h, l = logf_t.shape
    nblk = pl.cdiv(l, LANES)
    x = jnp.pad(logf_t, ((0, 0), (0, 0), (0, nblk * LANES - l))).reshape(b, h, nblk, LANES)
    spec = pl.BlockSpec((1, h, nblk, LANES), lambda i: (i, 0, 0, 0))
    bias, pieces = pl.pallas_call(
        _cumsum_kernel, grid=(b,), in_specs=[spec],
        out_specs=(spec, pl.BlockSpec((1, BIAS_PIECES, h, nblk, LANES), lambda i: (i, 0, 0, 0, 0))),
        out_shape=(jax.ShapeDtypeStruct(x.shape, F32),
                   jax.ShapeDtypeStruct((b, BIAS_PIECES, h, nblk, LANES), BF16)),
        compiler_params=_params(("arbitrary",)), name="cumsum",
    )(x)
    return bias.reshape(b, h, nblk * LANES)[:, :, :l], pieces.reshape(b, BIAS_PIECES, h, nblk * LANES)[:, :, :, :l]


def _convbranch_kernel(u_ref, hist_ref, w_ref, b_ref, g_ref, beta_ref, wo_ref, o_ref, xp_ref, uc_ref, xs_ref,
                       *, tt, rows):
    t = pl.program_id(1)

    @pl.when(t == 0)
    def _():
        xp_ref[0:CONV_HALO, :] = hist_ref[0]

    @pl.when(t > 0)
    def _():
        xp_ref[0:CONV_HALO, :] = xp_ref[tt:tt + CONV_HALO, :]

    xp_ref[CONV_HALO:CONV_HALO + tt, :] = u_ref[0]
    first = CONV_HALO - (CONV_K - 1)
    d = u_ref.shape[-1]

    span = CONV_HALO - SUBLANES + tt

    def lane_block(c, carry):
        c0 = pl.multiple_of(c * LANES, LANES)
        cols = pl.ds(c0, LANES)
        for r in range(1, SUBLANES):
            xs_ref[r, 0:span, :] = xp_ref[r:r + span, cols]
        for r0 in range(0, tt, rows):
            acc = jnp.broadcast_to(b_ref[:, cols], (rows, LANES))
            for k in range(CONV_K):
                shift = (first + k) % SUBLANES
                base = r0 + first + k - shift
                if shift == 0:
                    x = xp_ref[base:base + rows, cols]
                else:
                    x = xs_ref[shift, base:base + rows, :]
                acc = acc + w_ref[k:k + 1, cols] * x
            uc_ref[r0:r0 + rows, cols] = acc
        return carry

    lax.fori_loop(0, d // LANES, lane_block, 0)
    act = jax.nn.silu(_layer_norm(uc_ref[...], g_ref[...], beta_ref[...])).astype(BF16)
    o_ref[0] = jnp.dot(act, wo_ref[...], preferred_element_type=F32).astype(BF16)


def _convbranch(u, hist, dw_w, dw_b, ln_g, ln_b, w_out, tt):
    b, t, d = u.shape
    hist_pad = jnp.pad(hist.astype(F32), ((0, 0), (CONV_HALO - (CONV_K - 1), 0), (0, 0)))
    tile = pl.BlockSpec((1, tt, d), lambda i, j: (i, j, 0))
    return pl.pallas_call(
        functools.partial(_convbranch_kernel, tt=tt, rows=min(tt, 64)),
        grid=(b, t // tt),
        in_specs=[tile, pl.BlockSpec((1, CONV_HALO, d), lambda i, j: (i, 0, 0)),
                  _resident((CONV_K, d)), _resident((1, d)), _resident((1, d)), _resident((1, d)),
                  _resident((d, d))],
        out_specs=tile,
        out_shape=jax.ShapeDtypeStruct((b, t, d), BF16),
        scratch_shapes=[pltpu.VMEM((CONV_HALO + tt, d), F32), pltpu.VMEM((tt, d), F32),
                        pltpu.VMEM((SUBLANES, CONV_HALO - SUBLANES + tt, LANES), F32)],
        compiler_params=_params(("arbitrary", "arbitrary")),
        name="convbranch",
    )(u, hist_pad, dw_w, dw_b, ln_g, ln_b, w_out)


def _head_masks(q):
    lane = lax.broadcasted_iota(jnp.int32, q.shape, 1)
    return [jnp.where((lane >= h * HEAD_DIM) & (lane < (h + 1) * HEAD_DIM), q, jnp.zeros_like(q))
            for h in range(HEADS_PER_TILE)]


def _softmax_step(qh, k, v, bias, m_ref, l_ref, acc_ref, mask):
    s = lax.dot_general(qh, k, (((1,), (1,)), ((), ())), preferred_element_type=F32) + bias
    if mask is not None:
        s = jnp.where(mask, s, MASK_VALUE)
    m_old = m_ref[...]
    m_new = jnp.maximum(m_old, jnp.max(s, axis=1, keepdims=True))
    alpha = jnp.exp2(m_old - m_new)
    p = jnp.exp2(s - m_new)
    l_ref[...] = alpha * l_ref[...] + jnp.sum(p, axis=1, keepdims=True)
    acc_ref[...] = alpha * acc_ref[...] + jnp.dot(p.astype(BF16), v, preferred_element_type=F32)
    m_ref[...] = m_new


def _merge_heads(acc_refs, l_refs):
    lane = lax.broadcasted_iota(jnp.int32, acc_refs[0].shape, 1)
    out = acc_refs[0][...] / l_refs[0][...]
    for h in range(1, HEADS_PER_TILE):
        out = jnp.where(lane >= h * HEAD_DIM, acc_refs[h][...] / l_refs[h][...], out)
    return out


def _attn_prompt_kernel(q_ref, k_ref, bias_ref, vt_ref, o_ref, qa_ref, m_ref, acc_ref, sa_ref, sb_ref, ra_ref, rb_ref,
                        *, tile):
    g = pl.program_id(1)
    i = pl.program_id(2)
    s_refs = (sa_ref, sb_ref)
    r_refs = (ra_ref, rb_ref)
    lane = lax.broadcasted_iota(jnp.int32, (tile, LANES), 1)
    q = q_ref[0]
    for h in range(HEADS_PER_TILE):
        head = g * HEADS_PER_TILE + h
        rows = slice(h * tile, (h + 1) * tile)
        qa_ref[rows, 0:LANES] = jnp.where((lane >= h * HEAD_DIM) & (lane < (h + 1) * HEAD_DIM), q, jnp.zeros_like(q))
        qa_ref[rows, LANES:2 * LANES] = jnp.where(
            (lane >= BIAS_PIECES * head) & (lane < BIAS_PIECES * (head + 1)), 1.0, 0.0).astype(BF16)
    m_ref[...] = jnp.full(m_ref.shape, MASK_VALUE, F32)
    acc_ref[...] = jnp.zeros(acc_ref.shape, F32)
    ones = jnp.ones((ONES_ROWS, tile), BF16)

    def produce(j, slot, mask):
        k0 = pl.multiple_of(j * tile, tile)
        kb = jnp.concatenate([k_ref[0, pl.ds(k0, tile), :], bias_ref[0, pl.ds(k0, tile), :]], axis=1)
        s = lax.dot_general(kb, qa_ref[...], (((1,), (1,)), ((), ())), preferred_element_type=F32)
        if mask is not None:
            s = jnp.where(mask, s, MASK_VALUE)
        s_refs[slot][...] = s
        r_refs[slot][...] = jnp.max(s, axis=0, keepdims=True)

    def consume(j, slot):
        k0 = pl.multiple_of(j * tile, tile)
        m_old = m_ref[...]
        m_new = jnp.maximum(m_old, r_refs[slot][...])
        alpha = jnp.exp2(m_old - m_new)
        m_ref[...] = m_new
        p = jnp.exp2(s_refs[slot][...] - m_new).astype(BF16)
        for h in range(HEADS_PER_TILE):
            cols = slice(h * tile, (h + 1) * tile)
            vt1 = jnp.concatenate([vt_ref[0, h * HEAD_DIM:(h + 1) * HEAD_DIM, pl.ds(k0, tile)], ones], axis=0)
            acc_ref[h] = alpha[:, cols] * acc_ref[h] + jnp.dot(vt1, p[:, cols], preferred_element_type=F32)

    n_full = i
    key = lax.broadcasted_iota(jnp.int32, (tile, HEADS_PER_TILE * tile), 0)
    query = lax.broadcasted_iota(jnp.int32, (tile, HEADS_PER_TILE * tile), 1) % tile
    produce(n_full, 0, key <= query)
    held = lambda t: jnp.where(t == 0, n_full, t - 1)

    def two_steps(u, carry):
        t = 2 * u
        produce(t, 1, None)
        consume(held(t), 0)
        produce(t + 1, 0, None)
        consume(held(t + 1), 1)
        return carry

    lax.fori_loop(0, n_full // 2, two_steps, 0)
    t0 = 2 * (n_full // 2)

    @pl.when(n_full % 2 == 1)
    def _():
        produce(t0, 1, None)
        consume(held(t0), 0)
        consume(held(t0 + 1), 1)

    @pl.when(n_full % 2 == 0)
    def _():
        consume(held(t0), 0)

    out = [acc_ref[h][0:HEAD_DIM, :] / acc_ref[h][HEAD_DIM:HEAD_DIM + 1, :] for h in range(HEADS_PER_TILE)]
    o_ref[0] = jnp.transpose(jnp.concatenate(out, axis=0)).astype(BF16)


def _attn_prompt(q, k, bias, vt, tile):
    b, t, d = q.shape
    return pl.pallas_call(
        functools.partial(_attn_prompt_kernel, tile=tile),
        grid=(b, d // LANES, t // tile),
        in_specs=[pl.BlockSpec((1, tile, LANES), lambda bi, g, i: (bi, i, g)),
                  pl.BlockSpec((1, t, LANES), lambda bi, g, i: (bi, 0, g)),
                  pl.BlockSpec((1, t, LANES), lambda bi, g, i: (bi, 0, 0)),
                  pl.BlockSpec((1, LANES, t), lambda bi, g, i: (bi, g, 0))],
        out_specs=pl.BlockSpec((1, tile, LANES), lambda bi, g, i: (bi, i, g)),
        out_shape=jax.ShapeDtypeStruct((b, t, d), BF16),
        scratch_shapes=[pltpu.VMEM((HEADS_PER_TILE * tile, 2 * LANES), BF16),
                        pltpu.VMEM((1, HEADS_PER_TILE * tile), F32),
                        pltpu.VMEM((HEADS_PER_TILE, HEAD_DIM + ONES_ROWS, tile), F32),
                        pltpu.VMEM((tile, HEADS_PER_TILE * tile), F32), pltpu.VMEM((tile, HEADS_PER_TILE * tile), F32),
                        pltpu.VMEM((1, HEADS_PER_TILE * tile), F32), pltpu.VMEM((1, HEADS_PER_TILE * tile), F32)],
        compiler_params=_params(("arbitrary", "arbitrary", "arbitrary")),
        name="attn_prompt",
    )(q, k, bias, vt)


def _attn_sample_kernel(q_ref, kh_ref, vh_ref, nch_ref, kn_ref, vn_ref, ncn_ref, o_ref, m_ref, l_ref, acc_ref):
    j = pl.program_id(1)
    groups = q_ref.shape[-1] // LANES

    @pl.when(j == 0)
    def _():
        m_ref[...] = jnp.full(m_ref.shape, MASK_VALUE, F32)
        l_ref[...] = jnp.zeros(l_ref.shape, F32)
        acc_ref[...] = jnp.zeros(acc_ref.shape, F32)

    def update(k_ref_, v_ref_, nc_ref_, mask):
        for g in range(groups):
            cols = slice(g * LANES, (g + 1) * LANES)
            qh = _head_masks(q_ref[0, :, cols])
            k = k_ref_[0, :, cols].astype(BF16)
            v = v_ref_[0, :, cols].astype(BF16)
            for h in range(HEADS_PER_TILE):
                hh = g * HEADS_PER_TILE + h
                _softmax_step(qh[h], k, v, nc_ref_[0, hh:hh + 1, :], m_ref.at[hh], l_ref.at[hh], acc_ref.at[hh], mask)

    update(kh_ref, vh_ref, nch_ref, None)

    @pl.when(j == pl.num_programs(1) - 1)
    def _():
        t = q_ref.shape[1]
        row = lax.broadcasted_iota(jnp.int32, (t, t), 0)
        col = lax.broadcasted_iota(jnp.int32, (t, t), 1)
        update(kn_ref, vn_ref, ncn_ref, col <= row)
        for g in range(groups):
            hs = range(g * HEADS_PER_TILE, (g + 1) * HEADS_PER_TILE)
            o_ref[0, :, g * LANES:(g + 1) * LANES] = _merge_heads(
                [acc_ref.at[h] for h in hs], [l_ref.at[h] for h in hs]).astype(BF16)


def _attn_sample(q, k_new, v_new, neg_c_new, k_hist, v_hist, neg_c_hist, tk):
    b, t, d = q.shape
    p = k_hist.shape[1]
    new = pl.BlockSpec((1, t, d), lambda bi, j: (bi, 0, 0))
    hist = pl.BlockSpec((1, tk, d), lambda bi, j: (bi, j, 0))
    return pl.pallas_call(
        _attn_sample_kernel,
        grid=(b, p // tk),
        in_specs=[new, hist, hist, pl.BlockSpec((1, N_HEADS, tk), lambda bi, j: (bi, 0, j)),
                  new, new, pl.BlockSpec((1, N_HEADS, t), lambda bi, j: (bi, 0, 0))],
        out_specs=new,
        out_shape=jax.ShapeDtypeStruct((b, t, d), BF16),
        scratch_shapes=[pltpu.VMEM((N_HEADS, t, 1), F32), pltpu.VMEM((N_HEADS, t, 1), F32),
                        pltpu.VMEM((N_HEADS, t, LANES), F32)],
        compiler_params=_params(("arbitrary", "arbitrary")),
        name="attn_sample",
    )(q, k_hist, v_hist, neg_c_hist, k_new, v_new, neg_c_new)


def _merge_kernel(x_ref, g0_ref, b0_ref, cv_ref, o_ref, gc_ref, ga_ref, wao_ref, wo_ref, g1_ref, b1_ref,
                  y_ref, *, alpha):
    xn = _layer_norm(x_ref[...], g0_ref[...], b0_ref[...])
    attn_out = jnp.dot(o_ref[...], wao_ref[...], preferred_element_type=F32)
    mix = gc_ref[...].astype(F32) * cv_ref[...].astype(F32) + ga_ref[...].astype(F32) * attn_out
    y = jnp.dot(mix.astype(BF16), wo_ref[...], preferred_element_type=F32)
    y_ref[...] = _layer_norm(alpha * xn + y, g1_ref[...], b1_ref[...])


def _merge(x, ln0_g, ln0_b, conv_out, attn, gate_c, gate_a, w_attn_out, w_o, ln1_g, ln1_b, alpha, tm):
    n, d = x.shape
    row = pl.BlockSpec((tm, d), lambda i: (i, 0))
    vec = _resident((1, d))
    return pl.pallas_call(
        functools.partial(_merge_kernel, alpha=alpha),
        grid=(n // tm,),
        in_specs=[row, vec, vec, row, row, row, row, _resident((d, d)), _resident((d, d)), vec, vec],
        out_specs=row,
        out_shape=jax.ShapeDtypeStruct((n, d), F32),
        compiler_params=_params(("arbitrary",)),
        name="merge",
    )(x, ln0_g, ln0_b, conv_out, attn, gate_c, gate_a, w_attn_out, w_o, ln1_g, ln1_b)


def _ffn_kernel(x_ref, p_ref, hist_ref, wua_ref, wub_ref, dw_ref, db_ref, wd_ref, wpg_ref, wp_ref, g_ref, b_ref,
                y_ref, tail_ref, ap_ref, *, tt, alpha):
    t = pl.program_id(1)

    @pl.when(t == 0)
    def _():
        ap_ref[0:FFN_HALO, :] = hist_ref[0]

    @pl.when(t > 0)
    def _():
        ap_ref[0:FFN_HALO, :] = ap_ref[tt:tt + FFN_HALO, :]

    x = x_ref[0]
    xb = x.astype(BF16)
    ap_ref[FFN_HALO:FFN_HALO + tt, :] = jnp.dot(xb, wua_ref[...], preferred_element_type=F32)
    tail_ref[0] = ap_ref[tt:tt + FFN_HALO, :]
    first = FFN_HALO - (FFN_K - 1)
    ac = db_ref[...]
    for k in range(FFN_K):
        ac = ac + dw_ref[k:k + 1, :] * ap_ref[first + k:first + k + tt, :]
    gate = jnp.dot(xb, wub_ref[...], preferred_element_type=F32)
    f = jnp.dot((jax.nn.silu(ac) * gate).astype(BF16), wd_ref[...], preferred_element_type=F32)
    ple = (jax.nn.sigmoid(jnp.dot(xb, wpg_ref[...], preferred_element_type=F32))
           * jnp.dot(p_ref[0].astype(BF16), wp_ref[...], preferred_element_type=F32))
    y_ref[0] = _layer_norm(alpha * x + f + ple, g_ref[...], b_ref[...])


def _ffn(x, p, hist, w_up_a, w_up_b, dw_w, dw_b, w_down, w_pg, w_ple, ln_g, ln_b, alpha, tt):
    b, t, d = x.shape
    dff = w_up_a.shape[1]
    dp = p.shape[-1]
    hist_pad = jnp.pad(hist.astype(F32), ((0, 0), (FFN_HALO - (FFN_K - 1), 0), (0, 0)))
    tile = pl.BlockSpec((1, tt, d), lambda i, j: (i, j, 0))
    halo = pl.BlockSpec((1, FFN_HALO, dff), lambda i, j: (i, 0, 0))
    return pl.pallas_call(
        functools.partial(_ffn_kernel, tt=tt, alpha=alpha),
        grid=(b, t // tt),
        in_specs=[tile, pl.BlockSpec((1, tt, dp), lambda i, j: (i, j, 0)), halo,
                  _resident((d, dff)), _resident((d, dff)), _resident((FFN_K, dff)), _resident((1, dff)),
                  _resident((dff, d)), _resident((d, d)), _resident((dp, d)), _resident((1, d)), _resident((1, d))],
        out_specs=(tile, halo),
        out_shape=(jax.ShapeDtypeStruct((b, t, d), F32), jax.ShapeDtypeStruct((b, FFN_HALO, dff), F32)),
        scratch_shapes=[pltpu.VMEM((FFN_HALO + tt, dff), F32)],
        compiler_params=_params(("arbitrary", "arbitrary")),
        name="ffn",
    )(x, p, hist_pad, w_up_a, w_up_b, dw_w, dw_b, w_down, w_pg, w_ple, ln_g, ln_b)


def _tile(n, target):
    if n <= target:
        return n
    for cand in range(target, SUBLANES - 1, -SUBLANES):
        if n % cand == 0:
            return cand
    return n


def _layer(x, p, conv_hist, ffn_hist, cache, w, alpha):
    b, t, d = x.shape
    n = b * t
    assert t >= CONV_K - 1 and t % SUBLANES == 0
    tm = _tile(n, 512)
    tt = _tile(t, 256)
    prompt = cache is None
    if prompt:
        tile = _tile(t, 512)
        assert tile % LANES == 0 and tm % tile == 0 and t % tm == 0
    u, q, k, v, kb, vb, logf, gate_c, gate_a = _inproj(x.reshape(n, d), w["ln_in_g"], w["ln_in_b"], w["in"],
                                                       w["b_f"], tm, t, transpose_v=prompt)
    u = u.reshape(b, t, d)
    logf = logf.reshape(b, t, N_HEADS)
    conv_out = _convbranch(u, conv_hist, w["conv_dw_w"], w["conv_dw_b"], w["conv_ln_g"], w["conv_ln_b"],
                           w["conv_out"], tt)
    q3, k3 = q.reshape(b, t, d), kb.reshape(b, t, d)
    logf_t = jnp.swapaxes(logf, 1, 2)
    if prompt:
        _, pieces = _decay_bias(logf_t)
        lanes = jnp.transpose(pieces, (0, 3, 2, 1)).reshape(b, t, N_HEADS * BIAS_PIECES)
        bias = jnp.pad(lanes, ((0, 0), (0, 0), (0, LANES - N_HEADS * BIAS_PIECES)))
        attn = _attn_prompt(q3, k3, bias, vb, tile)
    else:
        k_hist, v_hist, logf_hist = cache
        past = k_hist.shape[1]
        bias, _ = _decay_bias(jnp.concatenate([jnp.swapaxes(logf_hist.astype(F32), 1, 2), logf_t], axis=2))
        attn = _attn_sample(q3, k3, vb.reshape(b, t, d), bias[:, :, past:], k_hist.reshape(b, past, d),
                            v_hist.reshape(b, past, d), bias[:, :, :past], _tile(past, 512))
    x1 = _merge(x.reshape(n, d), w["ln_in_g"], w["ln_in_b"], conv_out.reshape(n, d), attn.reshape(n, d),
                gate_c, gate_a, w["attn_out"], w["o"], w["ln1_g"], w["ln1_b"], alpha, tm)
    y, tail = _ffn(x1.reshape(b, t, d), p, ffn_hist, w["ffn_up_a"], w["ffn_up_b"], w["ffn_dw_w"], w["ffn_dw_b"],
                   w["ffn_down"], w["ple_gate"], w["ple"], w["ln2_g"], w["ln2_b"], alpha, tt)
    conv_new = u[:, t - (CONV_K - 1):]
    ffn_new = tail[:, FFN_HALO - (FFN_K - 1):]
    return (y, k.reshape(b, t, N_HEADS, HEAD_DIM), v.reshape(b, t, N_HEADS, HEAD_DIM), logf, conv_new, ffn_new)


def kernel(x_prompt, x_sample, cache_k, cache_v, cache_logf, state_conv, state_ffn_conv, p_prompt, p_sample, ln0_g, ln0_b, w_in, b_f, conv_dw_w, conv_dw_b, conv_ln_g, conv_ln_b, w_conv_out, w_attn_out, w_o, ln1_g, ln1_b, w_ffn_up, ffn_dw_w, ffn_dw_b, w_ffn_down, ln2_g, ln2_b, w_ple, w_ple_gate):
    depth = w_in.shape[0]
    assert depth == 1, "LN0 is fused into the first layer's projections; deeper stacks need an identity LN for later layers"
    alpha = (2.0 * depth) ** 0.25
    d = x_prompt.shape[-1]
    d_attn = N_HEADS * HEAD_DIM
    d_ff = w_ffn_up.shape[-1] // 2
    row = lambda a: a.reshape(1, -1).astype(F32)
    i = 0
    wi = w_in[i].astype(BF16)
    bounds = [0, d, 2 * d, 2 * d + d_attn, 2 * d + 2 * d_attn, 2 * d + 3 * d_attn, 2 * d + 3 * d_attn + N_HEADS,
              2 * d + 3 * d_attn + N_HEADS + d, 2 * d + 3 * d_attn + N_HEADS + 2 * d]
    names = ("a", "g", "q", "k", "v", "f", "gc", "ga")
    wup = w_ffn_up[i].astype(BF16)
    w = {
        "ln_in_g": row(ln0_g), "ln_in_b": row(ln0_b),
        "in": {nm: wi[:, bounds[j]:bounds[j + 1]] for j, nm in enumerate(names)},
        "b_f": row(b_f[i]),
        "conv_dw_w": conv_dw_w[i].astype(F32), "conv_dw_b": row(conv_dw_b[i]),
        "conv_ln_g": row(conv_ln_g[i]), "conv_ln_b": row(conv_ln_b[i]),
        "conv_out": w_conv_out[i].astype(BF16), "attn_out": w_attn_out[i].astype(BF16), "o": w_o[i].astype(BF16),
        "ln1_g": row(ln1_g[i]), "ln1_b": row(ln1_b[i]),
        "ffn_up_a": wup[:, :d_ff], "ffn_up_b": wup[:, d_ff:],
        "ffn_dw_w": ffn_dw_w[i].astype(F32), "ffn_dw_b": row(ffn_dw_b[i]),
        "ffn_down": w_ffn_down[i].astype(BF16), "ple_gate": w_ple_gate[i].astype(BF16), "ple": w_ple[i].astype(BF16),
        "ln2_g": row(ln2_g[i]), "ln2_b": row(ln2_b[i]),
    }
    bp = x_prompt.shape[0]
    zeros_conv = jnp.zeros((bp, CONV_K - 1, d), F32)
    zeros_ffn = jnp.zeros((bp, FFN_K - 1, d_ff), F32)
    yp, kp, vp, lp, cp, fp = _layer(x_prompt, p_prompt[i], zeros_conv, zeros_ffn, None, w, alpha)
    ys, ks, vs, ls, cs, fs = _layer(x_sample, p_sample[i], state_conv[i], state_ffn_conv[i],
                                    (cache_k[i], cache_v[i], cache_logf[i]), w, alpha)
    stack = lambda a: a[None]
    return (yp, ys, stack(kp), stack(vp), stack(lp), stack(cp), stack(fp),
            stack(ks), stack(vs), stack(ls), stack(cs), stack(fs))
```

```python
import functools

import jax
import jax.numpy as jnp
from jax import lax
from jax.experimental import pallas as pl
from jax.experimental.pallas import tpu as pltpu

N_HEADS = 16
HEAD_DIM = 64
CONV_K = 31
FFN_K = 3
LN_EPS = 1e-5

LANES = 128
SUBLANES = 8
HEADS_PER_TILE = LANES // HEAD_DIM
CONV_HALO = 32
FFN_HALO = SUBLANES
MASK_VALUE = -1e30
LOG2E = 1.4426950408889634
Q_SCALE = LOG2E * HEAD_DIM ** -0.5
BIAS_PIECES = 3
ONES_ROWS = 16
VMEM_LIMIT = 56 * 1024 * 1024

F32 = jnp.float32
BF16 = jnp.bfloat16


def _layer_norm(x, g, b):
    mu = jnp.mean(x, axis=-1, keepdims=True)
    xc = x - mu
    var = jnp.mean(xc * xc, axis=-1, keepdims=True)
    return xc * lax.rsqrt(var + LN_EPS) * g + b


def _resident(shape):
    return pl.BlockSpec(shape, lambda *_: (0,) * len(shape), pipeline_mode=pl.Buffered(1))


def _params(semantics):
    return pltpu.CompilerParams(dimension_semantics=semantics, vmem_limit_bytes=VMEM_LIMIT)


def _inproj_kernel(x_ref, g_ref, b_ref, wa_ref, wg_ref, wq_ref, wk_ref, wv_ref, wf_ref, wgc_ref,
                   wga_ref, bf_ref, u_ref, q_ref, kt_ref, vt_ref, lft_ref, gc_ref, ga_ref, *rest, ts):
    xn = _layer_norm(x_ref[...], g_ref[...], b_ref[...]).astype(BF16)

    def proj(w_ref):
        return jnp.dot(xn, w_ref[...], preferred_element_type=F32)

    u_ref[...] = proj(wa_ref) * jax.nn.sigmoid(proj(wg_ref))
    q_ref[...] = (proj(wq_ref) * Q_SCALE).astype(BF16)
    k = proj(wk_ref)
    kt = jnp.transpose(k)
    vt = jnp.transpose(proj(wv_ref))
    logits_t = jnp.transpose(proj(wf_ref))[0:N_HEADS, :]
    lft = jax.nn.log_sigmoid(logits_t + bf_ref[...])
    for s in range(xn.shape[0] // ts):
        cols = slice(s * ts, (s + 1) * ts)
        kt_ref[s] = kt[:, cols]
        vt_ref[s] = vt[:, cols]
        lft_ref[s] = lft[:, cols]
    if rest:
        kb_ref, vtb_ref = rest
        kb_ref[...] = k.astype(BF16)
        vtb_ref[0] = vt.astype(BF16)
    gc_ref[...] = jax.nn.sigmoid(proj(wgc_ref)).astype(BF16)
    ga_ref[...] = jax.nn.sigmoid(proj(wga_ref)).astype(BF16)


def _inproj(x, ln_g, ln_b, w, b_f, tm, t, bf16_copies):
    n, d = x.shape
    b = n // t
    ts = min(tm, t)
    row = lambda width: pl.BlockSpec((tm, width), lambda i: (i, 0))
    if tm <= t:
        tiles = t // tm
        tspec = lambda r: pl.BlockSpec((1, r, tm), lambda i: (i // tiles, 0, i % tiles))
    else:
        tspec = lambda r: pl.BlockSpec((tm // t, r, t), lambda i: (i, 0, 0))
    out_shape = [jax.ShapeDtypeStruct((n, d), F32), jax.ShapeDtypeStruct((n, d), BF16),
                 jax.ShapeDtypeStruct((b, d, t), F32), jax.ShapeDtypeStruct((b, d, t), F32),
                 jax.ShapeDtypeStruct((b, N_HEADS, t), F32),
                 jax.ShapeDtypeStruct((n, d), BF16), jax.ShapeDtypeStruct((n, d), BF16)]
    out_specs = [row(d), row(d), tspec(d), tspec(d), tspec(N_HEADS), row(d), row(d)]
    if bf16_copies:
        assert tm <= t
        out_shape += [jax.ShapeDtypeStruct((n, d), BF16), jax.ShapeDtypeStruct((b, d, t), BF16)]
        out_specs += [row(d), tspec(d)]
    return pl.pallas_call(
        functools.partial(_inproj_kernel, ts=ts),
        grid=(n // tm,),
        in_specs=[row(d), _resident((1, d)), _resident((1, d))]
                 + [_resident(w[name].shape) for name in ("a", "g", "q", "k", "v", "f", "gc", "ga")]
                 + [_resident((N_HEADS, 1))],
        out_specs=tuple(out_specs),
        out_shape=tuple(out_shape),
        compiler_params=_params(("arbitrary",)),
        name="inproj",
    )(x, ln_g, ln_b, w["a"], w["g"], w["q"], w["k"], w["v"], w["f"], w["gc"], w["ga"], b_f)


def _cumsum_kernel(x_ref, o_ref, p_ref):
    x = x_ref[0]
    nblk = x.shape[1]
    lane = lax.broadcasted_iota(jnp.int32, x.shape, 2)
    shift = 1
    while shift < LANES:
        x = x + jnp.where(lane >= shift, pltpu.roll(x, shift, 2), 0.0)
        shift *= 2
    tot = jnp.broadcast_to(x[:, :, LANES - 1:LANES], x.shape)
    blk = lax.broadcasted_iota(jnp.int32, x.shape, 1)
    inc = tot
    shift = 1
    while shift < nblk:
        inc = inc + jnp.where(blk >= shift, pltpu.roll(inc, shift, 1), 0.0)
        shift *= 2
    bias = (x + (inc - tot)) * (-LOG2E)
    o_ref[0] = bias
    rest = bias
    for piece in range(BIAS_PIECES):
        part = rest.astype(BF16)
        p_ref[0, piece] = part
        rest = rest - part.astype(F32)


def _decay_bias(logf_t):
    b, h, l = logf_t.shape
    nblk = pl.cdiv(l, LANES)
    x = jnp.pad(logf_t, ((0, 0), (0, 0), (0, nblk * LANES - l))).reshape(b, h, nblk, LANES)
    spec = pl.BlockSpec((1, h, nblk, LANES), lambda i: (i, 0, 0, 0))
    bias, pieces = pl.pallas_call(
        _cumsum_kernel, grid=(b,), in_specs=[spec],
        out_specs=(spec, pl.BlockSpec((1, BIAS_PIECES, h, nblk, LANES), lambda i: (i, 0, 0, 0, 0))),
        out_shape=(jax.ShapeDtypeStruct(x.shape, F32),
                   jax.ShapeDtypeStruct((b, BIAS_PIECES, h, nblk, LANES), BF16)),
        compiler_params=_params(("arbitrary",)), name="cumsum",
    )(x)
    return bias.reshape(b, h, nblk * LANES)[:, :, :l], pieces.reshape(b, BIAS_PIECES, h, nblk * LANES)[:, :, :, :l]


def _convbranch_kernel(u_ref, hist_ref, w_ref, b_ref, g_ref, beta_ref, wo_ref, o_ref, xp_ref, uc_ref, xs_ref,
                       *, tt, rows):
    t = pl.program_id(1)

    @pl.when(t == 0)
    def _():
        xp_ref[0:CONV_HALO, :] = hist_ref[0]

    @pl.when(t > 0)
    def _():
        xp_ref[0:CONV_HALO, :] = xp_ref[tt:tt + CONV_HALO, :]

    xp_ref[CONV_HALO:CONV_HALO + tt, :] = u_ref[0]
    first = CONV_HALO - (CONV_K - 1)
    d = u_ref.shape[-1]

    span = CONV_HALO - SUBLANES + tt

    def lane_block(c, carry):
        c0 = pl.multiple_of(c * LANES, LANES)
        cols = pl.ds(c0, LANES)
        for r in range(1, SUBLANES):
            xs_ref[r, 0:span, :] = xp_ref[r:r + span, cols]
        for r0 in range(0, tt, rows):
            acc = jnp.broadcast_to(b_ref[:, cols], (rows, LANES))
            for k in range(CONV_K):
                shift = (first + k) % SUBLANES
                base = r0 + first + k - shift
                if shift == 0:
                    x = xp_ref[base:base + rows, cols]
                else:
                    x = xs_ref[shift, base:base + rows, :]
                acc = acc + w_ref[k:k + 1, cols] * x
            uc_ref[r0:r0 + rows, cols] = acc
        return carry

    lax.fori_loop(0, d // LANES, lane_block, 0)
    act = jax.nn.silu(_layer_norm(uc_ref[...], g_ref[...], beta_ref[...])).astype(BF16)
    o_ref[0] = jnp.dot(act, wo_ref[...], preferred_element_type=F32).astype(BF16)


def _convbranch(u, hist, dw_w, dw_b, ln_g, ln_b, w_out, tt):
    b, t, d = u.shape
    hist_pad = jnp.pad(hist.astype(F32), ((0, 0), (CONV_HALO - (CONV_K - 1), 0), (0, 0)))
    tile = pl.BlockSpec((1, tt, d), lambda i, j: (i, j, 0))
    return pl.pallas_call(
        functools.partial(_convbranch_kernel, tt=tt, rows=min(tt, 64)),
        grid=(b, t // tt),
        in_specs=[tile, pl.BlockSpec((1, CONV_HALO, d), lambda i, j: (i, 0, 0)),
                  _resident((CONV_K, d)), _resident((1, d)), _resident((1, d)), _resident((1, d)),
                  _resident((d, d))],
        out_specs=tile,
        out_shape=jax.ShapeDtypeStruct((b, t, d), BF16),
        scratch_shapes=[pltpu.VMEM((CONV_HALO + tt, d), F32), pltpu.VMEM((tt, d), F32),
                        pltpu.VMEM((SUBLANES, CONV_HALO - SUBLANES + tt, LANES), F32)],
        compiler_params=_params(("arbitrary", "arbitrary")),
        name="convbranch",
    )(u, hist_pad, dw_w, dw_b, ln_g, ln_b, w_out)


def _attn_prompt_kernel(q_ref, k_ref, bias_ref, vt_ref, o_ref, qa_ref, m_ref, acc_ref, sa_ref, sb_ref, ra_ref, rb_ref,
                        *, tile):
    g = pl.program_id(1)
    i = pl.program_id(2)
    s_refs = (sa_ref, sb_ref)
    r_refs = (ra_ref, rb_ref)
    lane = lax.broadcasted_iota(jnp.int32, (tile, LANES), 1)
    q = q_ref[0]
    for h in range(HEADS_PER_TILE):
        head = g * HEADS_PER_TILE + h
        rows = slice(h * tile, (h + 1) * tile)
        qa_ref[rows, 0:LANES] = jnp.where((lane >= h * HEAD_DIM) & (lane < (h + 1) * HEAD_DIM), q, jnp.zeros_like(q))
        qa_ref[rows, LANES:2 * LANES] = jnp.where(
            (lane >= BIAS_PIECES * head) & (lane < BIAS_PIECES * (head + 1)), 1.0, 0.0).astype(BF16)
    m_ref[...] = jnp.full(m_ref.shape, MASK_VALUE, F32)
    acc_ref[...] = jnp.zeros(acc_ref.shape, F32)
    ones = jnp.ones((ONES_ROWS, tile), BF16)

    def produce(j, slot, mask):
        k0 = pl.multiple_of(j * tile, tile)
        kb = jnp.concatenate([k_ref[0, pl.ds(k0, tile), :], bias_ref[0, pl.ds(k0, tile), :]], axis=1)
        s = lax.dot_general(kb, qa_ref[...], (((1,), (1,)), ((), ())), preferred_element_type=F32)
        if mask is not None:
            s = jnp.where(mask, s, MASK_VALUE)
        s_refs[slot][...] = s
        r_refs[slot][...] = jnp.max(s, axis=0, keepdims=True)

    def consume(j, slot):
        k0 = pl.multiple_of(j * tile, tile)
        m_old = m_ref[...]
        m_new = jnp.maximum(m_old, r_refs[slot][...])
        alpha = jnp.exp2(m_old - m_new)
        m_ref[...] = m_new
        p = jnp.exp2(s_refs[slot][...] - m_new).astype(BF16)
        for h in range(HEADS_PER_TILE):
            cols = slice(h * tile, (h + 1) * tile)
            vt1 = jnp.concatenate([vt_ref[0, h * HEAD_DIM:(h + 1) * HEAD_DIM, pl.ds(k0, tile)], ones], axis=0)
            acc_ref[h] = alpha[:, cols] * acc_ref[h] + jnp.dot(vt1, p[:, cols], preferred_element_type=F32)

    n_full = i
    key = lax.broadcasted_iota(jnp.int32, (tile, HEADS_PER_TILE * tile), 0)
    query = lax.broadcasted_iota(jnp.int32, (tile, HEADS_PER_TILE * tile), 1) % tile
    produce(n_full, 0, key <= query)
    held = lambda t: jnp.where(t == 0, n_full, t - 1)

    def two_steps(u, carry):
        t = 2 * u
        produce(t, 1, None)
        consume(held(t), 0)
        produce(t + 1, 0, None)
        consume(held(t + 1), 1)
        return carry

    lax.fori_loop(0, n_full // 2, two_steps, 0)
    t0 = 2 * (n_full // 2)

    @pl.when(n_full % 2 == 1)
    def _():
        produce(t0, 1, None)
        consume(held(t0), 0)
        consume(held(t0 + 1), 1)

    @pl.when(n_full % 2 == 0)
    def _():
        consume(held(t0), 0)

    out = [acc_ref[h][0:HEAD_DIM, :] / acc_ref[h][HEAD_DIM:HEAD_DIM + 1, :] for h in range(HEADS_PER_TILE)]
    o_ref[0] = jnp.transpose(jnp.concatenate(out, axis=0)).astype(BF16)


def _attn_prompt(q, k, bias, vt, tile):
    b, t, d = q.shape
    return pl.pallas_call(
        functools.partial(_attn_prompt_kernel, tile=tile),
        grid=(b, d // LANES, t // tile),
        in_specs=[pl.BlockSpec((1, tile, LANES), lambda bi, g, i: (bi, i, g)),
                  pl.BlockSpec((1, t, LANES), lambda bi, g, i: (bi, 0, g)),
                  pl.BlockSpec((1, t, LANES), lambda bi, g, i: (bi, 0, 0)),
                  pl.BlockSpec((1, LANES, t), lambda bi, g, i: (bi, g, 0))],
        out_specs=pl.BlockSpec((1, tile, LANES), lambda bi, g, i: (bi, i, g)),
        out_shape=jax.ShapeDtypeStruct((b, t, d), BF16),
        scratch_shapes=[pltpu.VMEM((HEADS_PER_TILE * tile, 2 * LANES), BF16),
                        pltpu.VMEM((1, HEADS_PER_TILE * tile), F32),
                        pltpu.VMEM((HEADS_PER_TILE, HEAD_DIM + ONES_ROWS, tile), F32),
                        pltpu.VMEM((tile, HEADS_PER_TILE * tile), F32), pltpu.VMEM((tile, HEADS_PER_TILE * tile), F32),
                        pltpu.VMEM((1, HEADS_PER_TILE * tile), F32), pltpu.VMEM((1, HEADS_PER_TILE * tile), F32)],
        compiler_params=_params(("arbitrary", "arbitrary", "arbitrary")),
        name="attn_prompt",
    )(q, k, bias, vt)


def _attn_sample_kernel(q_ref, kth_ref, vth_ref, bh_ref, ktn_ref, vtn_ref, bn_ref, o_ref,
                        qh_ref, sh_ref, ph_ref, sn_ref, pn_ref, m_ref, acc_ref):
    j = pl.program_id(1)
    t = q_ref.shape[1]

    @pl.when(j == 0)
    def _():
        m_ref[...] = jnp.full(m_ref.shape, MASK_VALUE, F32)
        acc_ref[...] = jnp.zeros(acc_ref.shape, F32)
        for h in range(N_HEADS):
            qh_ref[h] = q_ref[0, :, h * HEAD_DIM:(h + 1) * HEAD_DIM]

    def update(kt_ref, vt_ref, bias_ref, s_ref, p_ref, mask):
        keys = kt_ref.shape[-1]
        for h in range(N_HEADS):
            s = jnp.dot(qh_ref[h], kt_ref[0, h].astype(BF16), preferred_element_type=F32) + bias_ref[0, h:h + 1, :]
            s_ref[h] = s if mask is None else jnp.where(mask, s, MASK_VALUE)
        s = s_ref[...]
        m_old = m_ref[...]
        m_new = jnp.maximum(m_old, jnp.max(s, axis=2, keepdims=True))
        alpha = jnp.exp2(m_old - m_new)
        m_ref[...] = m_new
        p_ref[...] = jnp.exp2(s - m_new).astype(BF16)
        ones = jnp.ones((HEAD_DIM, keys), BF16)
        for h in range(N_HEADS):
            vt1 = jnp.concatenate([vt_ref[0, h].astype(BF16), ones], axis=0)
            pv = lax.dot_general(p_ref[h], vt1, (((1,), (1,)), ((), ())), preferred_element_type=F32)
            acc_ref[h] = alpha[h] * acc_ref[h] + pv

    update(kth_ref, vth_ref, bh_ref, sh_ref, ph_ref, None)

    @pl.when(j == pl.num_programs(1) - 1)
    def _():
        row = lax.broadcasted_iota(jnp.int32, (t, t), 0)
        col = lax.broadcasted_iota(jnp.int32, (t, t), 1)
        update(ktn_ref, vtn_ref, bn_ref, sn_ref, pn_ref, col <= row)
        o_ref[0] = jnp.concatenate(
            [acc_ref[h][:, 0:HEAD_DIM] / acc_ref[h][:, HEAD_DIM:HEAD_DIM + 1] for h in range(N_HEADS)],
            axis=1).astype(BF16)


def _attn_sample(q, kt_new, vt_new, bias_new, kt_hist, vt_hist, bias_hist, tk):
    b, t, d = q.shape
    p = kt_hist.shape[-1]
    new = pl.BlockSpec((1, N_HEADS, HEAD_DIM, t), lambda bi, j: (bi, 0, 0, 0))
    hist = pl.BlockSpec((1, N_HEADS, HEAD_DIM, tk), lambda bi, j: (bi, 0, 0, j))
    rows = pl.BlockSpec((1, t, d), lambda bi, j: (bi, 0, 0))
    return pl.pallas_call(
        _attn_sample_kernel,
        grid=(b, p // tk),
        in_specs=[rows, hist, hist, pl.BlockSpec((1, N_HEADS, tk), lambda bi, j: (bi, 0, j)),
                  new, new, pl.BlockSpec((1, N_HEADS, t), lambda bi, j: (bi, 0, 0))],
        out_specs=rows,
        out_shape=jax.ShapeDtypeStruct((b, t, d), BF16),
        scratch_shapes=[pltpu.VMEM((N_HEADS, t, HEAD_DIM), BF16),
                        pltpu.VMEM((N_HEADS, t, tk), F32), pltpu.VMEM((N_HEADS, t, tk), BF16),
                        pltpu.VMEM((N_HEADS, t, t), F32), pltpu.VMEM((N_HEADS, t, t), BF16),
                        pltpu.VMEM((N_HEADS, t, 1), F32), pltpu.VMEM((N_HEADS, t, 2 * HEAD_DIM), F32)],
        compiler_params=_params(("arbitrary", "arbitrary")),
        name="attn_sample",
    )(q, kt_hist, vt_hist, bias_hist, kt_new, vt_new, bias_new)


def _merge_kernel(x_ref, g0_ref, b0_ref, cv_ref, o_ref, gc_ref, ga_ref, wao_ref, wo_ref, g1_ref, b1_ref,
                  y_ref, *, alpha):
    xn = _layer_norm(x_ref[...], g0_ref[...], b0_ref[...])
    attn_out = jnp.dot(o_ref[...], wao_ref[...], preferred_element_type=F32)
    mix = gc_ref[...].astype(F32) * cv_ref[...].astype(F32) + ga_ref[...].astype(F32) * attn_out
    y = jnp.dot(mix.astype(BF16), wo_ref[...], preferred_element_type=F32)
    y_ref[...] = _layer_norm(alpha * xn + y, g1_ref[...], b1_ref[...])


def _merge(x, ln0_g, ln0_b, conv_out, attn, gate_c, gate_a, w_attn_out, w_o, ln1_g, ln1_b, alpha, tm):
    n, d = x.shape
    row = pl.BlockSpec((tm, d), lambda i: (i, 0))
    vec = _resident((1, d))
    return pl.pallas_call(
        functools.partial(_merge_kernel, alpha=alpha),
        grid=(n // tm,),
        in_specs=[row, vec, vec, row, row, row, row, _resident((d, d)), _resident((d, d)), vec, vec],
        out_specs=row,
        out_shape=jax.ShapeDtypeStruct((n, d), F32),
        compiler_params=_params(("arbitrary",)),
        name="merge",
    )(x, ln0_g, ln0_b, conv_out, attn, gate_c, gate_a, w_attn_out, w_o, ln1_g, ln1_b)


def _ffn_kernel(x_ref, p_ref, hist_ref, wua_ref, wub_ref, dw_ref, db_ref, wd_ref, wpg_ref, wp_ref, g_ref, b_ref,
                y_ref, tail_ref, ap_ref, *, tt, alpha):
    t = pl.program_id(1)

    @pl.when(t == 0)
    def _():
        ap_ref[0:FFN_HALO, :] = hist_ref[0]

    @pl.when(t > 0)
    def _():
        ap_ref[0:FFN_HALO, :] = ap_ref[tt:tt + FFN_HALO, :]

    x = x_ref[0]
    xb = x.astype(BF16)
    ap_ref[FFN_HALO:FFN_HALO + tt, :] = jnp.dot(xb, wua_ref[...], preferred_element_type=F32)
    tail_ref[0] = ap_ref[tt:tt + FFN_HALO, :]
    first = FFN_HALO - (FFN_K - 1)
    ac = db_ref[...]
    for k in range(FFN_K):
        ac = ac + dw_ref[k:k + 1, :] * ap_ref[first + k:first + k + tt, :]
    gate = jnp.dot(xb, wub_ref[...], preferred_element_type=F32)
    f = jnp.dot((jax.nn.silu(ac) * gate).astype(BF16), wd_ref[...], preferred_element_type=F32)
    ple = (jax.nn.sigmoid(jnp.dot(xb, wpg_ref[...], preferred_element_type=F32))
           * jnp.dot(p_ref[0].astype(BF16), wp_ref[...], preferred_element_type=F32))
    y_ref[0] = _layer_norm(alpha * x + f + ple, g_ref[...], b_ref[...])


def _ffn(x, p, hist, w_up_a, w_up_b, dw_w, dw_b, w_down, w_pg, w_ple, ln_g, ln_b, alpha, tt):
    b, t, d = x.shape
    dff = w_up_a.shape[1]
    dp = p.shape[-1]
    hist_pad = jnp.pad(hist.astype(F32), ((0, 0), (FFN_HALO - (FFN_K - 1), 0), (0, 0)))
    tile = pl.BlockSpec((1, tt, d), lambda i, j: (i, j, 0))
    halo = pl.BlockSpec((1, FFN_HALO, dff), lambda i, j: (i, 0, 0))
    return pl.pallas_call(
        functools.partial(_ffn_kernel, tt=tt, alpha=alpha),
        grid=(b, t // tt),
        in_specs=[tile, pl.BlockSpec((1, tt, dp), lambda i, j: (i, j, 0)), halo,
                  _resident((d, dff)), _resident((d, dff)), _resident((FFN_K, dff)), _resident((1, dff)),
                  _resident((dff, d)), _resident((d, d)), _resident((dp, d)), _resident((1, d)), _resident((1, d))],
        out_specs=(tile, halo),
        out_shape=(jax.ShapeDtypeStruct((b, t, d), F32), jax.ShapeDtypeStruct((b, FFN_HALO, dff), F32)),
        scratch_shapes=[pltpu.VMEM((FFN_HALO + tt, dff), F32)],
        compiler_params=_params(("arbitrary", "arbitrary")),
        name="ffn",
    )(x, p, hist_pad, w_up_a, w_up_b, dw_w, dw_b, w_down, w_pg, w_ple, ln_g, ln_b)


def _tile(n, target):
    if n <= target:
        return n
    for cand in range(target, SUBLANES - 1, -SUBLANES):
        if n % cand == 0:
            return cand
    return n


def _layer(x, p, conv_hist, ffn_hist, cache, w, alpha):
    b, t, d = x.shape
    n = b * t
    assert t >= CONV_K - 1 and t % SUBLANES == 0
    tm = _tile(n, 512)
    tt = _tile(t, 256)
    prompt = cache is None
    if prompt:
        tile = _tile(t, 512)
        assert tile % LANES == 0 and tm % tile == 0 and t % tm == 0
    u, q, kt, vt, logf_t, gate_c, gate_a, *copies = _inproj(x.reshape(n, d), w["ln_in_g"], w["ln_in_b"], w["in"],
                                                            w["b_f"], tm, t, bf16_copies=prompt)
    u = u.reshape(b, t, d)
    conv_out = _convbranch(u, conv_hist, w["conv_dw_w"], w["conv_dw_b"], w["conv_ln_g"], w["conv_ln_b"],
                           w["conv_out"], tt)
    q3 = q.reshape(b, t, d)
    per_head = lambda a: a.reshape(b, N_HEADS, HEAD_DIM, a.shape[-1])
    if prompt:
        kb, vtb = copies
        _, pieces = _decay_bias(logf_t)
        lanes = jnp.transpose(pieces, (0, 3, 2, 1)).reshape(b, t, N_HEADS * BIAS_PIECES)
        bias = jnp.pad(lanes, ((0, 0), (0, 0), (0, LANES - N_HEADS * BIAS_PIECES)))
        attn = _attn_prompt(q3, kb.reshape(b, t, d), bias, vtb, tile)
    else:
        k_hist, v_hist, logf_hist = cache
        past = k_hist.shape[1]
        time_last = lambda a: jnp.transpose(a.astype(F32), (0, 2, 3, 1))
        bias, _ = _decay_bias(jnp.concatenate([jnp.swapaxes(logf_hist.astype(F32), 1, 2), logf_t], axis=2))
        attn = _attn_sample(q3, per_head(kt), per_head(vt), bias[:, :, past:], time_last(k_hist), time_last(v_hist),
                            bias[:, :, :past], _tile(past, 1024))
    x1 = _merge(x.reshape(n, d), w["ln_in_g"], w["ln_in_b"], conv_out.reshape(n, d), attn.reshape(n, d),
                gate_c, gate_a, w["attn_out"], w["o"], w["ln1_g"], w["ln1_b"], alpha, tm)
    y, tail = _ffn(x1.reshape(b, t, d), p, ffn_hist, w["ffn_up_a"], w["ffn_up_b"], w["ffn_dw_w"], w["ffn_dw_b"],
                   w["ffn_down"], w["ple_gate"], w["ple"], w["ln2_g"], w["ln2_b"], alpha, tt)
    conv_new = u[:, t - (CONV_K - 1):]
    ffn_new = tail[:, FFN_HALO - (FFN_K - 1):]
    frames_first = lambda a: jnp.transpose(per_head(a), (0, 3, 1, 2))
    return (y, frames_first(kt), frames_first(vt), jnp.swapaxes(logf_t, 1, 2), conv_new, ffn_new)


def kernel(x_prompt, x_sample, cache_k, cache_v, cache_logf, state_conv, state_ffn_conv, p_prompt, p_sample, ln0_g, ln0_b, w_in, b_f, conv_dw_w, conv_dw_b, conv_ln_g, conv_ln_b, w_conv_out, w_attn_out, w_o, ln1_g, ln1_b, w_ffn_up, ffn_dw_w, ffn_dw_b, w_ffn_down, ln2_g, ln2_b, w_ple, w_ple_gate):
    depth = w_in.shape[0]
    assert depth == 1, "LN0 is fused into the first layer's projections; deeper stacks need an identity LN for later layers"
    alpha = (2.0 * depth) ** 0.25
    d = x_prompt.shape[-1]
    d_attn = N_HEADS * HEAD_DIM
    d_ff = w_ffn_up.shape[-1] // 2
    row = lambda a: a.reshape(1, -1).astype(F32)
    i = 0
    wi = w_in[i].astype(BF16)
    bounds = [0, d, 2 * d, 2 * d + d_attn, 2 * d + 2 * d_attn, 2 * d + 3 * d_attn, 2 * d + 3 * d_attn + N_HEADS,
              2 * d + 3 * d_attn + N_HEADS + d, 2 * d + 3 * d_attn + N_HEADS + 2 * d]
    names = ("a", "g", "q", "k", "v", "f", "gc", "ga")
    wup = w_ffn_up[i].astype(BF16)
    w_in_parts = {nm: wi[:, bounds[j]:bounds[j + 1]] for j, nm in enumerate(names)}
    w_in_parts["f"] = jnp.pad(w_in_parts["f"], ((0, 0), (0, LANES - N_HEADS)))
    w = {
        "ln_in_g": row(ln0_g), "ln_in_b": row(ln0_b),
        "in": w_in_parts,
        "b_f": b_f[i].reshape(-1, 1).astype(F32),
        "conv_dw_w": conv_dw_w[i].astype(F32), "conv_dw_b": row(conv_dw_b[i]),
        "conv_ln_g": row(conv_ln_g[i]), "conv_ln_b": row(conv_ln_b[i]),
        "conv_out": w_conv_out[i].astype(BF16), "attn_out": w_attn_out[i].astype(BF16), "o": w_o[i].astype(BF16),
        "ln1_g": row(ln1_g[i]), "ln1_b": row(ln1_b[i]),
        "ffn_up_a": wup[:, :d_ff], "ffn_up_b": wup[:, d_ff:],
        "ffn_dw_w": ffn_dw_w[i].astype(F32), "ffn_dw_b": row(ffn_dw_b[i]),
        "ffn_down": w_ffn_down[i].astype(BF16), "ple_gate": w_ple_gate[i].astype(BF16), "ple": w_ple[i].astype(BF16),
        "ln2_g": row(ln2_g[i]), "ln2_b": row(ln2_b[i]),
    }
    bp = x_prompt.shape[0]
    zeros_conv = jnp.zeros((bp, CONV_K - 1, d), F32)
    zeros_ffn = jnp.zeros((bp, FFN_K - 1, d_ff), F32)
    yp, kp, vp, lp, cp, fp = _layer(x_prompt, p_prompt[i], zeros_conv, zeros_ffn, None, w, alpha)
    ys, ks, vs, ls, cs, fs = _layer(x_sample, p_sample[i], state_conv[i], state_ffn_conv[i],
                                    (cache_k[i], cache_v[i], cache_logf[i]), w, alpha)
    stack = lambda a: a[None]
    return (yp, ys, stack(kp), stack(vp), stack(lp), stack(cp), stack(fp),
            stack(ks), stack(vs), stack(ls), stack(cs), stack(fs))
```

```python
import functools

import jax
import jax.numpy as jnp
from jax import lax
from jax.experimental import pallas as pl
from jax.experimental.pallas import tpu as pltpu

N_HEADS = 16
HEAD_DIM = 64
CONV_K = 31
FFN_K = 3
LN_EPS = 1e-5

LANES = 128
SUBLANES = 8
HEADS_PER_TILE = LANES // HEAD_DIM
CONV_HALO = 32
FFN_HALO = SUBLANES
MASK_VALUE = -1e30
LOG2E = 1.4426950408889634
Q_SCALE = LOG2E * HEAD_DIM ** -0.5
BIAS_PIECES = 3
ONES_ROWS = 16
PIPELINE_UNROLL = 4
VMEM_LIMIT = 56 * 1024 * 1024

F32 = jnp.float32
BF16 = jnp.bfloat16


def _layer_norm(x, g, b):
    mu = jnp.mean(x, axis=-1, keepdims=True)
    xc = x - mu
    var = jnp.mean(xc * xc, axis=-1, keepdims=True)
    return xc * lax.rsqrt(var + LN_EPS) * g + b


def _resident(shape):
    return pl.BlockSpec(shape, lambda *_: (0,) * len(shape), pipeline_mode=pl.Buffered(1))


def _params(semantics):
    return pltpu.CompilerParams(dimension_semantics=semantics, vmem_limit_bytes=VMEM_LIMIT)


def _inproj_kernel(x_ref, g_ref, b_ref, wa_ref, wg_ref, wq_ref, wk_ref, wv_ref, wf_ref, wgc_ref,
                   wga_ref, bf_ref, u_ref, q_ref, kt_ref, vt_ref, lft_ref, gc_ref, ga_ref, *rest, ts):
    xn = _layer_norm(x_ref[...], g_ref[...], b_ref[...]).astype(BF16)

    def proj(w_ref):
        return jnp.dot(xn, w_ref[...], preferred_element_type=F32)

    u_ref[...] = proj(wa_ref) * jax.nn.sigmoid(proj(wg_ref))
    q_ref[...] = (proj(wq_ref) * Q_SCALE).astype(BF16)
    k = proj(wk_ref)
    kt = jnp.transpose(k)
    vt = jnp.transpose(proj(wv_ref))
    logits_t = jnp.transpose(proj(wf_ref))[0:N_HEADS, :]
    lft = jax.nn.log_sigmoid(logits_t + bf_ref[...])
    for s in range(xn.shape[0] // ts):
        cols = slice(s * ts, (s + 1) * ts)
        kt_ref[s] = kt[:, cols]
        vt_ref[s] = vt[:, cols]
        lft_ref[s] = lft[:, cols]
    if rest:
        kb_ref, vtb_ref = rest
        kb_ref[...] = k.astype(BF16)
        vtb_ref[0] = vt.astype(BF16)
    gc_ref[...] = jax.nn.sigmoid(proj(wgc_ref)).astype(BF16)
    ga_ref[...] = jax.nn.sigmoid(proj(wga_ref)).astype(BF16)


def _inproj(x, ln_g, ln_b, w, b_f, tm, t, bf16_copies):
    n, d = x.shape
    b = n // t
    ts = min(tm, t)
    row = lambda width: pl.BlockSpec((tm, width), lambda i: (i, 0))
    if tm <= t:
        tiles = t // tm
        tspec = lambda r: pl.BlockSpec((1, r, tm), lambda i: (i // tiles, 0, i % tiles))
    else:
        tspec = lambda r: pl.BlockSpec((tm // t, r, t), lambda i: (i, 0, 0))
    out_shape = [jax.ShapeDtypeStruct((n, d), F32), jax.ShapeDtypeStruct((n, d), BF16),
                 jax.ShapeDtypeStruct((b, d, t), F32), jax.ShapeDtypeStruct((b, d, t), F32),
                 jax.ShapeDtypeStruct((b, N_HEADS, t), F32),
                 jax.ShapeDtypeStruct((n, d), BF16), jax.ShapeDtypeStruct((n, d), BF16)]
    out_specs = [row(d), row(d), tspec(d), tspec(d), tspec(N_HEADS), row(d), row(d)]
    if bf16_copies:
        assert tm <= t
        out_shape += [jax.ShapeDtypeStruct((n, d), BF16), jax.ShapeDtypeStruct((b, d, t), BF16)]
        out_specs += [row(d), tspec(d)]
    return pl.pallas_call(
        functools.partial(_inproj_kernel, ts=ts),
        grid=(n // tm,),
        in_specs=[row(d), _resident((1, d)), _resident((1, d))]
                 + [_resident(w[name].shape) for name in ("a", "g", "q", "k", "v", "f", "gc", "ga")]
                 + [_resident((N_HEADS, 1))],
        out_specs=tuple(out_specs),
        out_shape=tuple(out_shape),
        compiler_params=_params(("arbitrary",)),
        name="inproj",
    )(x, ln_g, ln_b, w["a"], w["g"], w["q"], w["k"], w["v"], w["f"], w["gc"], w["ga"], b_f)


def _cumsum_kernel(x_ref, o_ref, p_ref):
    x = x_ref[0]
    nblk = x.shape[1]
    lane = lax.broadcasted_iota(jnp.int32, x.shape, 2)
    shift = 1
    while shift < LANES:
        x = x + jnp.where(lane >= shift, pltpu.roll(x, shift, 2), 0.0)
        shift *= 2
    tot = jnp.broadcast_to(x[:, :, LANES - 1:LANES], x.shape)
    blk = lax.broadcasted_iota(jnp.int32, x.shape, 1)
    inc = tot
    shift = 1
    while shift < nblk:
        inc = inc + jnp.where(blk >= shift, pltpu.roll(inc, shift, 1), 0.0)
        shift *= 2
    bias = (x + (inc - tot)) * (-LOG2E)
    o_ref[0] = bias
    rest = bias
    for piece in range(BIAS_PIECES):
        part = rest.astype(BF16)
        p_ref[0, piece] = part
        rest = rest - part.astype(F32)


def _decay_bias(logf_t):
    b, h, l = logf_t.shape
    nblk = pl.cdiv(l, LANES)
    x = jnp.pad(logf_t, ((0, 0), (0, 0), (0, nblk * LANES - l))).reshape(b, h, nblk, LANES)
    spec = pl.BlockSpec((1, h, nblk, LANES), lambda i: (i, 0, 0, 0))
    bias, pieces = pl.pallas_call(
        _cumsum_kernel, grid=(b,), in_specs=[spec],
        out_specs=(spec, pl.BlockSpec((1, BIAS_PIECES, h, nblk, LANES), lambda i: (i, 0, 0, 0, 0))),
        out_shape=(jax.ShapeDtypeStruct(x.shape, F32),
                   jax.ShapeDtypeStruct((b, BIAS_PIECES, h, nblk, LANES), BF16)),
        compiler_params=_params(("arbitrary",)), name="cumsum",
    )(x)
    return bias.reshape(b, h, nblk * LANES)[:, :, :l], pieces.reshape(b, BIAS_PIECES, h, nblk * LANES)[:, :, :, :l]


def _convbranch_kernel(u_ref, hist_ref, w_ref, b_ref, g_ref, beta_ref, wo_ref, o_ref, xp_ref, uc_ref, xs_ref,
                       *, tt, rows):
    t = pl.program_id(1)

    @pl.when(t == 0)
    def _():
        xp_ref[0:CONV_HALO, :] = hist_ref[0]

    @pl.when(t > 0)
    def _():
        xp_ref[0:CONV_HALO, :] = xp_ref[tt:tt + CONV_HALO, :]

    xp_ref[CONV_HALO:CONV_HALO + tt, :] = u_ref[0]
    first = CONV_HALO - (CONV_K - 1)
    d = u_ref.shape[-1]

    span = CONV_HALO - SUBLANES + tt

    def lane_block(c, carry):
        c0 = pl.multiple_of(c * LANES, LANES)
        cols = pl.ds(c0, LANES)
        for r in range(1, SUBLANES):
            xs_ref[r, 0:span, :] = xp_ref[r:r + span, cols]
        for r0 in range(0, tt, rows):
            acc = jnp.broadcast_to(b_ref[:, cols], (rows, LANES))
            for k in range(CONV_K):
                shift = (first + k) % SUBLANES
                base = r0 + first + k - shift
                if shift == 0:
                    x = xp_ref[base:base + rows, cols]
                else:
                    x = xs_ref[shift, base:base + rows, :]
                acc = acc + w_ref[k:k + 1, cols] * x
            uc_ref[r0:r0 + rows, cols] = acc
        return carry

    lax.fori_loop(0, d // LANES, lane_block, 0)
    act = jax.nn.silu(_layer_norm(uc_ref[...], g_ref[...], beta_ref[...])).astype(BF16)
    o_ref[0] = jnp.dot(act, wo_ref[...], preferred_element_type=F32).astype(BF16)


def _convbranch(u, hist, dw_w, dw_b, ln_g, ln_b, w_out, tt):
    b, t, d = u.shape
    hist_pad = jnp.pad(hist.astype(F32), ((0, 0), (CONV_HALO - (CONV_K - 1), 0), (0, 0)))
    tile = pl.BlockSpec((1, tt, d), lambda i, j: (i, j, 0))
    return pl.pallas_call(
        functools.partial(_convbranch_kernel, tt=tt, rows=min(tt, 64)),
        grid=(b, t // tt),
        in_specs=[tile, pl.BlockSpec((1, CONV_HALO, d), lambda i, j: (i, 0, 0)),
                  _resident((CONV_K, d)), _resident((1, d)), _resident((1, d)), _resident((1, d)),
                  _resident((d, d))],
        out_specs=tile,
        out_shape=jax.ShapeDtypeStruct((b, t, d), BF16),
        scratch_shapes=[pltpu.VMEM((CONV_HALO + tt, d), F32), pltpu.VMEM((tt, d), F32),
                        pltpu.VMEM((SUBLANES, CONV_HALO - SUBLANES + tt, LANES), F32)],
        compiler_params=_params(("arbitrary", "arbitrary")),
        name="convbranch",
    )(u, hist_pad, dw_w, dw_b, ln_g, ln_b, w_out)


def _attn_prompt_kernel(q_ref, k_ref, bias_ref, vt_ref, o_ref, qa_ref, m_ref, acc_ref, sa_ref, sb_ref, ra_ref, rb_ref,
                        *, tile):
    g = pl.program_id(1)
    i = pl.program_id(2)
    s_refs = (sa_ref, sb_ref)
    r_refs = (ra_ref, rb_ref)
    lane = lax.broadcasted_iota(jnp.int32, (tile, LANES), 1)
    q = q_ref[0]
    for h in range(HEADS_PER_TILE):
        head = g * HEADS_PER_TILE + h
        rows = slice(h * tile, (h + 1) * tile)
        qa_ref[rows, 0:LANES] = jnp.where((lane >= h * HEAD_DIM) & (lane < (h + 1) * HEAD_DIM), q, jnp.zeros_like(q))
        qa_ref[rows, LANES:2 * LANES] = jnp.where(
            (lane >= BIAS_PIECES * head) & (lane < BIAS_PIECES * (head + 1)), 1.0, 0.0).astype(BF16)
    m_ref[...] = jnp.full(m_ref.shape, MASK_VALUE, F32)
    acc_ref[...] = jnp.zeros(acc_ref.shape, F32)
    ones = jnp.ones((ONES_ROWS, tile), BF16)

    def produce(j, slot, mask):
        k0 = pl.multiple_of(j * tile, tile)
        kb = jnp.concatenate([k_ref[0, pl.ds(k0, tile), :], bias_ref[0, pl.ds(k0, tile), :]], axis=1)
        s = lax.dot_general(kb, qa_ref[...], (((1,), (1,)), ((), ())), preferred_element_type=F32)
        if mask is not None:
            s = jnp.concatenate([jnp.where(mask, s[:, h * tile:(h + 1) * tile], MASK_VALUE)
                                 for h in range(HEADS_PER_TILE)], axis=1)
        s_refs[slot][...] = s
        r_refs[slot][...] = jnp.max(s, axis=0, keepdims=True)

    def consume(j, slot):
        k0 = pl.multiple_of(j * tile, tile)
        m_old = m_ref[...]
        m_new = jnp.maximum(m_old, r_refs[slot][...])
        alpha = jnp.exp2(m_old - m_new)
        m_ref[...] = m_new
        p = jnp.exp2(s_refs[slot][...] - m_new).astype(BF16)
        for h in range(HEADS_PER_TILE):
            cols = slice(h * tile, (h + 1) * tile)
            vt1 = jnp.concatenate([vt_ref[0, h * HEAD_DIM:(h + 1) * HEAD_DIM, pl.ds(k0, tile)], ones], axis=0)
            acc_ref[h] = alpha[:, cols] * acc_ref[h] + jnp.dot(vt1, p[:, cols], preferred_element_type=F32)

    n_full = i
    key = lax.broadcasted_iota(jnp.int32, (tile, tile), 0)
    query = lax.broadcasted_iota(jnp.int32, (tile, tile), 1)
    produce(n_full, 0, key <= query)
    held = lambda t: jnp.where(t == 0, n_full, t - 1)

    def steps(t, count):
        for c in range(count):
            produce(t + c, (c + 1) % 2, None)
            consume(held(t + c), c % 2)

    def unrolled_steps(u, carry):
        steps(PIPELINE_UNROLL * u, PIPELINE_UNROLL)
        return carry

    lax.fori_loop(0, n_full // PIPELINE_UNROLL, unrolled_steps, 0)

    @pl.when(n_full % PIPELINE_UNROLL >= 2)
    def _():
        steps(PIPELINE_UNROLL * (n_full // PIPELINE_UNROLL), 2)

    t0 = 2 * (n_full // 2)

    @pl.when(n_full % 2 == 1)
    def _():
        steps(t0, 1)
        consume(held(t0 + 1), 1)

    @pl.when(n_full % 2 == 0)
    def _():
        consume(held(t0), 0)

    out = [acc_ref[h][0:HEAD_DIM, :] / acc_ref[h][HEAD_DIM:HEAD_DIM + 1, :] for h in range(HEADS_PER_TILE)]
    o_ref[0] = jnp.transpose(jnp.concatenate(out, axis=0)).astype(BF16)


def _attn_prompt(q, k, bias, vt, tile):
    b, t, d = q.shape
    return pl.pallas_call(
        functools.partial(_attn_prompt_kernel, tile=tile),
        grid=(b, d // LANES, t // tile),
        in_specs=[pl.BlockSpec((1, tile, LANES), lambda bi, g, i: (bi, i, g)),
                  pl.BlockSpec((1, t, LANES), lambda bi, g, i: (bi, 0, g)),
                  pl.BlockSpec((1, t, LANES), lambda bi, g, i: (bi, 0, 0)),
                  pl.BlockSpec((1, LANES, t), lambda bi, g, i: (bi, g, 0))],
        out_specs=pl.BlockSpec((1, tile, LANES), lambda bi, g, i: (bi, i, g)),
        out_shape=jax.ShapeDtypeStruct((b, t, d), BF16),
        scratch_shapes=[pltpu.VMEM((HEADS_PER_TILE * tile, 2 * LANES), BF16),
                        pltpu.VMEM((1, HEADS_PER_TILE * tile), F32),
                        pltpu.VMEM((HEADS_PER_TILE, HEAD_DIM + ONES_ROWS, tile), F32),
                        pltpu.VMEM((tile, HEADS_PER_TILE * tile), F32), pltpu.VMEM((tile, HEADS_PER_TILE * tile), F32),
                        pltpu.VMEM((1, HEADS_PER_TILE * tile), F32), pltpu.VMEM((1, HEADS_PER_TILE * tile), F32)],
        compiler_params=_params(("arbitrary", "arbitrary", "arbitrary")),
        name="attn_prompt",
    )(q, k, bias, vt)


def _attn_sample_kernel(q_ref, kth_ref, vth_ref, bh_ref, ktn_ref, vtn_ref, bn_ref, o_ref,
                        qh_ref, sh_ref, ph_ref, sn_ref, pn_ref, m_ref, acc_ref):
    j = pl.program_id(1)
    t = q_ref.shape[1]

    @pl.when(j == 0)
    def _():
        m_ref[...] = jnp.full(m_ref.shape, MASK_VALUE, F32)
        acc_ref[...] = jnp.zeros(acc_ref.shape, F32)
        for h in range(N_HEADS):
            qh_ref[h] = q_ref[0, :, h * HEAD_DIM:(h + 1) * HEAD_DIM]

    def update(kt_ref, vt_ref, bias_ref, s_ref, p_ref, mask):
        keys = kt_ref.shape[-1]
        for h in range(N_HEADS):
            s = jnp.dot(qh_ref[h], kt_ref[0, h].astype(BF16), preferred_element_type=F32) + bias_ref[0, h:h + 1, :]
            s_ref[h] = s if mask is None else jnp.where(mask, s, MASK_VALUE)
        s = s_ref[...]
        m_old = m_ref[...]
        m_new = jnp.maximum(m_old, jnp.max(s, axis=2, keepdims=True))
        alpha = jnp.exp2(m_old - m_new)
        m_ref[...] = m_new
        p_ref[...] = jnp.exp2(s - m_new).astype(BF16)
        ones = jnp.ones((HEAD_DIM, keys), BF16)
        for h in range(N_HEADS):
            vt1 = jnp.concatenate([vt_ref[0, h].astype(BF16), ones], axis=0)
            pv = lax.dot_general(p_ref[h], vt1, (((1,), (1,)), ((), ())), preferred_element_type=F32)
            acc_ref[h] = alpha[h] * acc_ref[h] + pv

    update(kth_ref, vth_ref, bh_ref, sh_ref, ph_ref, None)

    @pl.when(j == pl.num_programs(1) - 1)
    def _():
        row = lax.broadcasted_iota(jnp.int32, (t, t), 0)
        col = lax.broadcasted_iota(jnp.int32, (t, t), 1)
        update(ktn_ref, vtn_ref, bn_ref, sn_ref, pn_ref, col <= row)
        o_ref[0] = jnp.concatenate(
            [acc_ref[h][:, 0:HEAD_DIM] / acc_ref[h][:, HEAD_DIM:HEAD_DIM + 1] for h in range(N_HEADS)],
            axis=1).astype(BF16)


def _attn_sample(q, kt_new, vt_new, bias_new, kt_hist, vt_hist, bias_hist, tk):
    b, t, d = q.shape
    p = kt_hist.shape[-1]
    new = pl.BlockSpec((1, N_HEADS, HEAD_DIM, t), lambda bi, j: (bi, 0, 0, 0))
    hist = pl.BlockSpec((1, N_HEADS, HEAD_DIM, tk), lambda bi, j: (bi, 0, 0, j))
    rows = pl.BlockSpec((1, t, d), lambda bi, j: (bi, 0, 0))
    return pl.pallas_call(
        _attn_sample_kernel,
        grid=(b, p // tk),
        in_specs=[rows, hist, hist, pl.BlockSpec((1, N_HEADS, tk), lambda bi, j: (bi, 0, j)),
                  new, new, pl.BlockSpec((1, N_HEADS, t), lambda bi, j: (bi, 0, 0))],
        out_specs=rows,
        out_shape=jax.ShapeDtypeStruct((b, t, d), BF16),
        scratch_shapes=[pltpu.VMEM((N_HEADS, t, HEAD_DIM), BF16),
                        pltpu.VMEM((N_HEADS, t, tk), F32), pltpu.VMEM((N_HEADS, t, tk), BF16),
                        pltpu.VMEM((N_HEADS, t, t), F32), pltpu.VMEM((N_HEADS, t, t), BF16),
                        pltpu.VMEM((N_HEADS, t, 1), F32), pltpu.VMEM((N_HEADS, t, 2 * HEAD_DIM), F32)],
        compiler_params=_params(("arbitrary", "arbitrary")),
        name="attn_sample",
    )(q, kt_hist, vt_hist, bias_hist, kt_new, vt_new, bias_new)


def _merge_kernel(x_ref, g0_ref, b0_ref, cv_ref, o_ref, gc_ref, ga_ref, wao_ref, wo_ref, g1_ref, b1_ref,
                  y_ref, *, alpha):
    xn = _layer_norm(x_ref[...], g0_ref[...], b0_ref[...])
    attn_out = jnp.dot(o_ref[...], wao_ref[...], preferred_element_type=F32)
    mix = gc_ref[...].astype(F32) * cv_ref[...].astype(F32) + ga_ref[...].astype(F32) * attn_out
    y = jnp.dot(mix.astype(BF16), wo_ref[...], preferred_element_type=F32)
    y_ref[...] = _layer_norm(alpha * xn + y, g1_ref[...], b1_ref[...])


def _merge(x, ln0_g, ln0_b, conv_out, attn, gate_c, gate_a, w_attn_out, w_o, ln1_g, ln1_b, alpha, tm):
    n, d = x.shape
    row = pl.BlockSpec((tm, d), lambda i: (i, 0))
    vec = _resident((1, d))
    return pl.pallas_call(
        functools.partial(_merge_kernel, alpha=alpha),
        grid=(n // tm,),
        in_specs=[row, vec, vec, row, row, row, row, _resident((d, d)), _resident((d, d)), vec, vec],
        out_specs=row,
        out_shape=jax.ShapeDtypeStruct((n, d), F32),
        compiler_params=_params(("arbitrary",)),
        name="merge",
    )(x, ln0_g, ln0_b, conv_out, attn, gate_c, gate_a, w_attn_out, w_o, ln1_g, ln1_b)


def _ffn_kernel(x_ref, p_ref, hist_ref, wua_ref, wub_ref, dw_ref, db_ref, wd_ref, wpg_ref, wp_ref, g_ref, b_ref,
                y_ref, tail_ref, ap_ref, *, tt, alpha):
    t = pl.program_id(1)

    @pl.when(t == 0)
    def _():
        ap_ref[0:FFN_HALO, :] = hist_ref[0]

    @pl.when(t > 0)
    def _():
        ap_ref[0:FFN_HALO, :] = ap_ref[tt:tt + FFN_HALO, :]

    x = x_ref[0]
    xb = x.astype(BF16)
    ap_ref[FFN_HALO:FFN_HALO + tt, :] = jnp.dot(xb, wua_ref[...], preferred_element_type=F32)
    tail_ref[0] = ap_ref[tt:tt + FFN_HALO, :]
    first = FFN_HALO - (FFN_K - 1)
    ac = db_ref[...]
    for k in range(FFN_K):
        ac = ac + dw_ref[k:k + 1, :] * ap_ref[first + k:first + k + tt, :]
    gate = jnp.dot(xb, wub_ref[...], preferred_element_type=F32)
    f = jnp.dot((jax.nn.silu(ac) * gate).astype(BF16), wd_ref[...], preferred_element_type=F32)
    ple = (jax.nn.sigmoid(jnp.dot(xb, wpg_ref[...], preferred_element_type=F32))
           * jnp.dot(p_ref[0].astype(BF16), wp_ref[...], preferred_element_type=F32))
    y_ref[0] = _layer_norm(alpha * x + f + ple, g_ref[...], b_ref[...])


def _ffn(x, p, hist, w_up_a, w_up_b, dw_w, dw_b, w_down, w_pg, w_ple, ln_g, ln_b, alpha, tt):
    b, t, d = x.shape
    dff = w_up_a.shape[1]
    dp = p.shape[-1]
    hist_pad = jnp.pad(hist.astype(F32), ((0, 0), (FFN_HALO - (FFN_K - 1), 0), (0, 0)))
    tile = pl.BlockSpec((1, tt, d), lambda i, j: (i, j, 0))
    halo = pl.BlockSpec((1, FFN_HALO, dff), lambda i, j: (i, 0, 0))
    return pl.pallas_call(
        functools.partial(_ffn_kernel, tt=tt, alpha=alpha),
        grid=(b, t // tt),
        in_specs=[tile, pl.BlockSpec((1, tt, dp), lambda i, j: (i, j, 0)), halo,
                  _resident((d, dff)), _resident((d, dff)), _resident((FFN_K, dff)), _resident((1, dff)),
                  _resident((dff, d)), _resident((d, d)), _resident((dp, d)), _resident((1, d)), _resident((1, d))],
        out_specs=(tile, halo),
        out_shape=(jax.ShapeDtypeStruct((b, t, d), F32), jax.ShapeDtypeStruct((b, FFN_HALO, dff), F32)),
        scratch_shapes=[pltpu.VMEM((FFN_HALO + tt, dff), F32)],
        compiler_params=_params(("arbitrary", "arbitrary")),
        name="ffn",
    )(x, p, hist_pad, w_up_a, w_up_b, dw_w, dw_b, w_down, w_pg, w_ple, ln_g, ln_b)


def _tile(n, target):
    if n <= target:
        return n
    for cand in range(target, SUBLANES - 1, -SUBLANES):
        if n % cand == 0:
            return cand
    return n


def _layer(x, p, conv_hist, ffn_hist, cache, w, alpha):
    b, t, d = x.shape
    n = b * t
    assert t >= CONV_K - 1 and t % SUBLANES == 0
    tm = _tile(n, 512)
    tt = _tile(t, 256)
    prompt = cache is None
    if prompt:
        tile = _tile(t, 512)
        assert tile % LANES == 0 and tm % tile == 0 and t % tm == 0
    u, q, kt, vt, logf_t, gate_c, gate_a, *copies = _inproj(x.reshape(n, d), w["ln_in_g"], w["ln_in_b"], w["in"],
                                                            w["b_f"], tm, t, bf16_copies=prompt)
    u = u.reshape(b, t, d)
    conv_out = _convbranch(u, conv_hist, w["conv_dw_w"], w["conv_dw_b"], w["conv_ln_g"], w["conv_ln_b"],
                           w["conv_out"], _tile(t, 512))
    q3 = q.reshape(b, t, d)
    per_head = lambda a: a.reshape(b, N_HEADS, HEAD_DIM, a.shape[-1])
    if prompt:
        kb, vtb = copies
        _, pieces = _decay_bias(logf_t)
        lanes = jnp.transpose(pieces, (0, 3, 2, 1)).reshape(b, t, N_HEADS * BIAS_PIECES)
        bias = jnp.pad(lanes, ((0, 0), (0, 0), (0, LANES - N_HEADS * BIAS_PIECES)))
        attn = _attn_prompt(q3, kb.reshape(b, t, d), bias, vtb, tile)
    else:
        k_hist, v_hist, logf_hist = cache
        past = k_hist.shape[1]
        time_last = lambda a: jnp.transpose(a.astype(F32), (0, 2, 3, 1))
        bias, _ = _decay_bias(jnp.concatenate([jnp.swapaxes(logf_hist.astype(F32), 1, 2), logf_t], axis=2))
        attn = _attn_sample(q3, per_head(kt), per_head(vt), bias[:, :, past:], time_last(k_hist), time_last(v_hist),
                            bias[:, :, :past], _tile(past, 1024))
    x1 = _merge(x.reshape(n, d), w["ln_in_g"], w["ln_in_b"], conv_out.reshape(n, d), attn.reshape(n, d),
                gate_c, gate_a, w["attn_out"], w["o"], w["ln1_g"], w["ln1_b"], alpha, tm)
    y, tail = _ffn(x1.reshape(b, t, d), p, ffn_hist, w["ffn_up_a"], w["ffn_up_b"], w["ffn_dw_w"], w["ffn_dw_b"],
                   w["ffn_down"], w["ple_gate"], w["ple"], w["ln2_g"], w["ln2_b"], alpha, tt)
    conv_new = u[:, t - (CONV_K - 1):]
    ffn_new = tail[:, FFN_HALO - (FFN_K - 1):]
    frames_first = lambda a: jnp.transpose(per_head(a), (0, 3, 1, 2))
    return (y, frames_first(kt), frames_first(vt), jnp.swapaxes(logf_t, 1, 2), conv_new, ffn_new)


def kernel(x_prompt, x_sample, cache_k, cache_v, cache_logf, state_conv, state_ffn_conv, p_prompt, p_sample, ln0_g, ln0_b, w_in, b_f, conv_dw_w, conv_dw_b, conv_ln_g, conv_ln_b, w_conv_out, w_attn_out, w_o, ln1_g, ln1_b, w_ffn_up, ffn_dw_w, ffn_dw_b, w_ffn_down, ln2_g, ln2_b, w_ple, w_ple_gate):
    depth = w_in.shape[0]
    assert depth == 1, "LN0 is fused into the first layer's projections; deeper stacks need an identity LN for later layers"
    alpha = (2.0 * depth) ** 0.25
    d = x_prompt.shape[-1]
    d_attn = N_HEADS * HEAD_DIM
    d_ff = w_ffn_up.shape[-1] // 2
    row = lambda a: a.reshape(1, -1).astype(F32)
    i = 0
    wi = w_in[i].astype(BF16)
    bounds = [0, d, 2 * d, 2 * d + d_attn, 2 * d + 2 * d_attn, 2 * d + 3 * d_attn, 2 * d + 3 * d_attn + N_HEADS,
              2 * d + 3 * d_attn + N_HEADS + d, 2 * d + 3 * d_attn + N_HEADS + 2 * d]
    names = ("a", "g", "q", "k", "v", "f", "gc", "ga")
    wup = w_ffn_up[i].astype(BF16)
    w_in_parts = {nm: wi[:, bounds[j]:bounds[j + 1]] for j, nm in enumerate(names)}
    w_in_parts["f"] = jnp.pad(w_in_parts["f"], ((0, 0), (0, LANES - N_HEADS)))
    w = {
        "ln_in_g": row(ln0_g), "ln_in_b": row(ln0_b),
        "in": w_in_parts,
        "b_f": b_f[i].reshape(-1, 1).astype(F32),
        "conv_dw_w": conv_dw_w[i].astype(F32), "conv_dw_b": row(conv_dw_b[i]),
        "conv_ln_g": row(conv_ln_g[i]), "conv_ln_b": row(conv_ln_b[i]),
        "conv_out": w_conv_out[i].astype(BF16), "attn_out": w_attn_out[i].astype(BF16), "o": w_o[i].astype(BF16),
        "ln1_g": row(ln1_g[i]), "ln1_b": row(ln1_b[i]),
        "ffn_up_a": wup[:, :d_ff], "ffn_up_b": wup[:, d_ff:],
        "ffn_dw_w": ffn_dw_w[i].astype(F32), "ffn_dw_b": row(ffn_dw_b[i]),
        "ffn_down": w_ffn_down[i].astype(BF16), "ple_gate": w_ple_gate[i].astype(BF16), "ple": w_ple[i].astype(BF16),
        "ln2_g": row(ln2_g[i]), "ln2_b": row(ln2_b[i]),
    }
    bp = x_prompt.shape[0]
    zeros_conv = jnp.zeros((bp, CONV_K - 1, d), F32)
    zeros_ffn = jnp.zeros((bp, FFN_K - 1, d_ff), F32)
    yp, kp, vp, lp, cp, fp = _layer(x_prompt, p_prompt[i], zeros_conv, zeros_ffn, None, w, alpha)
    ys, ks, vs, ls, cs, fs = _layer(x_sample, p_sample[i], state_conv[i], state_ffn_conv[i],
                                    (cache_k[i], cache_v[i], cache_logf[i]), w, alpha)
    stack = lambda a: a[None]
    return (yp, ys, stack(kp), stack(vp), stack(lp), stack(cp), stack(fp),
            stack(ks), stack(vs), stack(ls), stack(cs), stack(fs))
```

```python
import functools

import jax
import jax.numpy as jnp
from jax import lax
from jax.experimental import pallas as pl
from jax.experimental.pallas import tpu as pltpu

N_HEADS = 16
HEAD_DIM = 64
CONV_K = 31
FFN_K = 3
LN_EPS = 1e-5

LANES = 128
SUBLANES = 8
HEADS_PER_TILE = LANES // HEAD_DIM
CONV_HALO = 32
FFN_HALO = SUBLANES
MASK_VALUE = -1e30
LOG2E = 1.4426950408889634
Q_SCALE = LOG2E * HEAD_DIM ** -0.5
BIAS_PIECES = 3
ONES_ROWS = 16
LANE_TILES_PER_STEP = 2
ROW_CHUNKS = 2
PIPELINE_UNROLL = 4
VMEM_LIMIT = 56 * 1024 * 1024

F32 = jnp.float32
BF16 = jnp.bfloat16


def _layer_norm(x, g, b):
    mu = jnp.mean(x, axis=-1, keepdims=True)
    xc = x - mu
    var = jnp.mean(xc * xc, axis=-1, keepdims=True)
    return xc * lax.rsqrt(var + LN_EPS) * g + b


def _resident(shape):
    return pl.BlockSpec(shape, lambda *_: (0,) * len(shape), pipeline_mode=pl.Buffered(1))


def _params(semantics):
    return pltpu.CompilerParams(dimension_semantics=semantics, vmem_limit_bytes=VMEM_LIMIT)


def _inproj_kernel(x_ref, g_ref, b_ref, wa_ref, wg_ref, wq_ref, wk_ref, wv_ref, wf_ref, wgc_ref,
                   wga_ref, bf_ref, u_ref, q_ref, kt_ref, vt_ref, lft_ref, gc_ref, ga_ref, *rest, ts):
    rc = x_ref.shape[0] // ROW_CHUNKS
    piece = min(ts, rc)
    for c in range(ROW_CHUNKS):
        rows = slice(c * rc, (c + 1) * rc)
        xn = _layer_norm(x_ref[rows, :], g_ref[...], b_ref[...]).astype(BF16)

        def proj(w_ref):
            return jnp.dot(xn, w_ref[...], preferred_element_type=F32)

        u_ref[rows, :] = proj(wa_ref) * jax.nn.sigmoid(proj(wg_ref))
        q_ref[rows, :] = (proj(wq_ref) * Q_SCALE).astype(BF16)
        k = proj(wk_ref)
        kt = jnp.transpose(k)
        vt = jnp.transpose(proj(wv_ref))
        logits_t = jnp.transpose(proj(wf_ref))[0:N_HEADS, :]
        lft = jax.nn.log_sigmoid(logits_t + bf_ref[...])
        for s in range(rc // piece):
            src = slice(s * piece, (s + 1) * piece)
            first = c * rc + s * piece
            dst = (first // ts, slice(None), slice(first % ts, first % ts + piece))
            kt_ref[dst] = kt[:, src]
            vt_ref[dst] = vt[:, src]
            lft_ref[dst] = lft[:, src]
        if rest:
            kb_ref, vtb_ref = rest
            kb_ref[rows, :] = k.astype(BF16)
            vtb_ref[0, :, rows] = vt.astype(BF16)
        gc_ref[rows, :] = jax.nn.sigmoid(proj(wgc_ref)).astype(BF16)
        ga_ref[rows, :] = jax.nn.sigmoid(proj(wga_ref)).astype(BF16)


def _inproj(x, ln_g, ln_b, w, b_f, tm, t, bf16_copies):
    n, d = x.shape
    b = n // t
    ts = min(tm, t)
    row = lambda width: pl.BlockSpec((tm, width), lambda i: (i, 0))
    if tm <= t:
        tiles = t // tm
        tspec = lambda r: pl.BlockSpec((1, r, tm), lambda i: (i // tiles, 0, i % tiles))
    else:
        tspec = lambda r: pl.BlockSpec((tm // t, r, t), lambda i: (i, 0, 0))
    out_shape = [jax.ShapeDtypeStruct((n, d), F32), jax.ShapeDtypeStruct((n, d), BF16),
                 jax.ShapeDtypeStruct((b, d, t), F32), jax.ShapeDtypeStruct((b, d, t), F32),
                 jax.ShapeDtypeStruct((b, N_HEADS, t), F32),
                 jax.ShapeDtypeStruct((n, d), BF16), jax.ShapeDtypeStruct((n, d), BF16)]
    out_specs = [row(d), row(d), tspec(d), tspec(d), tspec(N_HEADS), row(d), row(d)]
    if bf16_copies:
        assert tm <= t
        out_shape += [jax.ShapeDtypeStruct((n, d), BF16), jax.ShapeDtypeStruct((b, d, t), BF16)]
        out_specs += [row(d), tspec(d)]
    return pl.pallas_call(
        functools.partial(_inproj_kernel, ts=ts),
        grid=(n // tm,),
        in_specs=[row(d), _resident((1, d)), _resident((1, d))]
                 + [_resident(w[name].shape) for name in ("a", "g", "q", "k", "v", "f", "gc", "ga")]
                 + [_resident((N_HEADS, 1))],
        out_specs=tuple(out_specs),
        out_shape=tuple(out_shape),
        compiler_params=_params(("arbitrary",)),
        name="inproj",
    )(x, ln_g, ln_b, w["a"], w["g"], w["q"], w["k"], w["v"], w["f"], w["gc"], w["ga"], b_f)


def _cumsum_kernel(x_ref, o_ref, p_ref):
    x = x_ref[0]
    nblk = x.shape[1]
    lane = lax.broadcasted_iota(jnp.int32, x.shape, 2)
    shift = 1
    while shift < LANES:
        x = x + jnp.where(lane >= shift, pltpu.roll(x, shift, 2), 0.0)
        shift *= 2
    tot = jnp.broadcast_to(x[:, :, LANES - 1:LANES], x.shape)
    blk = lax.broadcasted_iota(jnp.int32, x.shape, 1)
    inc = tot
    shift = 1
    while shift < nblk:
        inc = inc + jnp.where(blk >= shift, pltpu.roll(inc, shift, 1), 0.0)
        shift *= 2
    bias = (x + (inc - tot)) * (-LOG2E)
    o_ref[0] = bias
    rest = bias
    for piece in range(BIAS_PIECES):
        part = rest.astype(BF16)
        p_ref[0, piece] = part
        rest = rest - part.astype(F32)


def _decay_bias(logf_t):
    b, h, l = logf_t.shape
    nblk = pl.cdiv(l, LANES)
    x = jnp.pad(logf_t, ((0, 0), (0, 0), (0, nblk * LANES - l))).reshape(b, h, nblk, LANES)
    spec = pl.BlockSpec((1, h, nblk, LANES), lambda i: (i, 0, 0, 0))
    bias, pieces = pl.pallas_call(
        _cumsum_kernel, grid=(b,), in_specs=[spec],
        out_specs=(spec, pl.BlockSpec((1, BIAS_PIECES, h, nblk, LANES), lambda i: (i, 0, 0, 0, 0))),
        out_shape=(jax.ShapeDtypeStruct(x.shape, F32),
                   jax.ShapeDtypeStruct((b, BIAS_PIECES, h, nblk, LANES), BF16)),
        compiler_params=_params(("arbitrary",)), name="cumsum",
    )(x)
    return bias.reshape(b, h, nblk * LANES)[:, :, :l], pieces.reshape(b, BIAS_PIECES, h, nblk * LANES)[:, :, :, :l]


def _convbranch_kernel(u_ref, hist_ref, w_ref, b_ref, g_ref, beta_ref, wo_ref, o_ref, xp_ref, uc_ref, xs_ref,
                       *, tt, rows):
    t = pl.program_id(1)

    @pl.when(t == 0)
    def _():
        xp_ref[0:CONV_HALO, :] = hist_ref[0]

    @pl.when(t > 0)
    def _():
        xp_ref[0:CONV_HALO, :] = xp_ref[tt:tt + CONV_HALO, :]

    xp_ref[CONV_HALO:CONV_HALO + tt, :] = u_ref[0]
    first = CONV_HALO - (CONV_K - 1)
    d = u_ref.shape[-1]

    span = CONV_HALO - SUBLANES + tt

    def lane_block(c, carry):
        c0 = pl.multiple_of(c * LANES, LANES)
        cols = pl.ds(c0, LANES)
        for r in range(1, SUBLANES):
            xs_ref[r, 0:span, :] = xp_ref[r:r + span, cols]
        for r0 in range(0, tt, rows):
            acc = jnp.broadcast_to(b_ref[:, cols], (rows, LANES))
            for k in range(CONV_K):
                shift = (first + k) % SUBLANES
                base = r0 + first + k - shift
                if shift == 0:
                    x = xp_ref[base:base + rows, cols]
                else:
                    x = xs_ref[shift, base:base + rows, :]
                acc = acc + w_ref[k:k + 1, cols] * x
            uc_ref[r0:r0 + rows, cols] = acc
        return carry

    lax.fori_loop(0, d // LANES, lane_block, 0)
    act = jax.nn.silu(_layer_norm(uc_ref[...], g_ref[...], beta_ref[...])).astype(BF16)
    o_ref[0] = jnp.dot(act, wo_ref[...], preferred_element_type=F32).astype(BF16)


def _convbranch(u, hist, dw_w, dw_b, ln_g, ln_b, w_out, tt):
    b, t, d = u.shape
    hist_pad = jnp.pad(hist.astype(F32), ((0, 0), (CONV_HALO - (CONV_K - 1), 0), (0, 0)))
    tile = pl.BlockSpec((1, tt, d), lambda i, j: (i, j, 0))
    return pl.pallas_call(
        functools.partial(_convbranch_kernel, tt=tt, rows=min(tt, 64)),
        grid=(b, t // tt),
        in_specs=[tile, pl.BlockSpec((1, CONV_HALO, d), lambda i, j: (i, 0, 0)),
                  _resident((CONV_K, d)), _resident((1, d)), _resident((1, d)), _resident((1, d)),
                  _resident((d, d))],
        out_specs=tile,
        out_shape=jax.ShapeDtypeStruct((b, t, d), BF16),
        scratch_shapes=[pltpu.VMEM((CONV_HALO + tt, d), F32), pltpu.VMEM((tt, d), F32),
                        pltpu.VMEM((SUBLANES, CONV_HALO - SUBLANES + tt, LANES), F32)],
        compiler_params=_params(("arbitrary", "arbitrary")),
        name="convbranch",
    )(u, hist_pad, dw_w, dw_b, ln_g, ln_b, w_out)


def _attn_prompt_kernel(q_ref, k_ref, bias_ref, vt_ref, o_ref, qa_ref, m_ref, acc_ref, sa_ref, sb_ref, ra_ref, rb_ref,
                        *, tile):
    g = pl.program_id(1)
    i = pl.program_id(2)
    s_refs = (sa_ref, sb_ref)
    r_refs = (ra_ref, rb_ref)
    lane = lax.broadcasted_iota(jnp.int32, (tile, LANES), 1)
    heads = LANE_TILES_PER_STEP * HEADS_PER_TILE
    width = HEADS_PER_TILE * tile
    for lt in range(LANE_TILES_PER_STEP):
        q = q_ref[0, :, lt * LANES:(lt + 1) * LANES]
        for hh in range(HEADS_PER_TILE):
            h = lt * HEADS_PER_TILE + hh
            head = g * heads + h
            rows = slice(h * tile, (h + 1) * tile)
            qa_ref[rows, 0:LANES] = jnp.where((lane >= hh * HEAD_DIM) & (lane < (hh + 1) * HEAD_DIM), q,
                                              jnp.zeros_like(q))
            qa_ref[rows, LANES:2 * LANES] = jnp.where(
                (lane >= BIAS_PIECES * head) & (lane < BIAS_PIECES * (head + 1)), 1.0, 0.0).astype(BF16)
    m_ref[...] = jnp.full(m_ref.shape, MASK_VALUE, F32)
    acc_ref[...] = jnp.zeros(acc_ref.shape, F32)
    ones = jnp.ones((ONES_ROWS, tile), BF16)

    def produce(j, slot, mask):
        k0 = pl.multiple_of(j * tile, tile)
        bias = bias_ref[0, pl.ds(k0, tile), :]
        for lt in range(LANE_TILES_PER_STEP):
            cols = slice(lt * width, (lt + 1) * width)
            kb = jnp.concatenate([k_ref[0, pl.ds(k0, tile), lt * LANES:(lt + 1) * LANES], bias], axis=1)
            s = lax.dot_general(kb, qa_ref[cols, :], (((1,), (1,)), ((), ())), preferred_element_type=F32)
            if mask is not None:
                s = jnp.concatenate([jnp.where(mask, s[:, h * tile:(h + 1) * tile], MASK_VALUE)
                                     for h in range(HEADS_PER_TILE)], axis=1)
            s_refs[slot][:, cols] = s
            r_refs[slot][:, cols] = jnp.max(s, axis=0, keepdims=True)

    def consume(j, slot):
        k0 = pl.multiple_of(j * tile, tile)
        m_old = m_ref[...]
        m_new = jnp.maximum(m_old, r_refs[slot][...])
        alpha = jnp.exp2(m_old - m_new)
        m_ref[...] = m_new
        p = jnp.exp2(s_refs[slot][...] - m_new).astype(BF16)
        for h in range(heads):
            cols = slice(h * tile, (h + 1) * tile)
            vt1 = jnp.concatenate([vt_ref[0, h * HEAD_DIM:(h + 1) * HEAD_DIM, pl.ds(k0, tile)], ones], axis=0)
            acc_ref[h] = alpha[:, cols] * acc_ref[h] + jnp.dot(vt1, p[:, cols], preferred_element_type=F32)

    n_full = i
    key = lax.broadcasted_iota(jnp.int32, (tile, tile), 0)
    query = lax.broadcasted_iota(jnp.int32, (tile, tile), 1)
    produce(n_full, 0, key <= query)
    held = lambda t: jnp.where(t == 0, n_full, t - 1)

    def steps(t, count):
        for c in range(count):
            produce(t + c, (c + 1) % 2, None)
            consume(held(t + c), c % 2)

    def unrolled_steps(u, carry):
        steps(PIPELINE_UNROLL * u, PIPELINE_UNROLL)
        return carry

    lax.fori_loop(0, n_full // PIPELINE_UNROLL, unrolled_steps, 0)

    @pl.when(n_full % PIPELINE_UNROLL >= 2)
    def _():
        steps(PIPELINE_UNROLL * (n_full // PIPELINE_UNROLL), 2)

    t0 = 2 * (n_full // 2)

    @pl.when(n_full % 2 == 1)
    def _():
        steps(t0, 1)
        consume(held(t0 + 1), 1)

    @pl.when(n_full % 2 == 0)
    def _():
        consume(held(t0), 0)

    out = [acc_ref[h][0:HEAD_DIM, :] / acc_ref[h][HEAD_DIM:HEAD_DIM + 1, :] for h in range(heads)]
    o_ref[0] = jnp.transpose(jnp.concatenate(out, axis=0)).astype(BF16)


def _attn_prompt(q, k, bias, vt, tile):
    b, t, d = q.shape
    w = LANE_TILES_PER_STEP * LANES
    heads = LANE_TILES_PER_STEP * HEADS_PER_TILE
    assert d % w == 0
    return pl.pallas_call(
        functools.partial(_attn_prompt_kernel, tile=tile),
        grid=(b, d // w, t // tile),
        in_specs=[pl.BlockSpec((1, tile, w), lambda bi, g, i: (bi, i, g)),
                  pl.BlockSpec((1, t, w), lambda bi, g, i: (bi, 0, g)),
                  pl.BlockSpec((1, t, LANES), lambda bi, g, i: (bi, 0, 0)),
                  pl.BlockSpec((1, w, t), lambda bi, g, i: (bi, g, 0))],
        out_specs=pl.BlockSpec((1, tile, w), lambda bi, g, i: (bi, i, g)),
        out_shape=jax.ShapeDtypeStruct((b, t, d), BF16),
        scratch_shapes=[pltpu.VMEM((heads * tile, 2 * LANES), BF16),
                        pltpu.VMEM((1, heads * tile), F32),
                        pltpu.VMEM((heads, HEAD_DIM + ONES_ROWS, tile), F32),
                        pltpu.VMEM((tile, heads * tile), F32), pltpu.VMEM((tile, heads * tile), F32),
                        pltpu.VMEM((1, heads * tile), F32), pltpu.VMEM((1, heads * tile), F32)],
        compiler_params=_params(("arbitrary", "arbitrary", "arbitrary")),
        name="attn_prompt",
    )(q, k, bias, vt)


def _attn_sample_kernel(q_ref, kth_ref, vth_ref, bh_ref, ktn_ref, vtn_ref, bn_ref, o_ref,
                        qh_ref, sh_ref, ph_ref, sn_ref, pn_ref, m_ref, acc_ref):
    j = pl.program_id(1)
    t = q_ref.shape[1]

    @pl.when(j == 0)
    def _():
        m_ref[...] = jnp.full(m_ref.shape, MASK_VALUE, F32)
        acc_ref[...] = jnp.zeros(acc_ref.shape, F32)
        for h in range(N_HEADS):
            qh_ref[h] = q_ref[0, :, h * HEAD_DIM:(h + 1) * HEAD_DIM]

    def update(kt_ref, vt_ref, bias_ref, s_ref, p_ref, mask):
        keys = kt_ref.shape[-1]
        for h in range(N_HEADS):
            s = jnp.dot(qh_ref[h], kt_ref[0, h].astype(BF16), preferred_element_type=F32) + bias_ref[0, h:h + 1, :]
            s_ref[h] = s if mask is None else jnp.where(mask, s, MASK_VALUE)
        s = s_ref[...]
        m_old = m_ref[...]
        m_new = jnp.maximum(m_old, jnp.max(s, axis=2, keepdims=True))
        alpha = jnp.exp2(m_old - m_new)
        m_ref[...] = m_new
        p_ref[...] = jnp.exp2(s - m_new).astype(BF16)
        ones = jnp.ones((HEAD_DIM, keys), BF16)
        for h in range(N_HEADS):
            vt1 = jnp.concatenate([vt_ref[0, h].astype(BF16), ones], axis=0)
            pv = lax.dot_general(p_ref[h], vt1, (((1,), (1,)), ((), ())), preferred_element_type=F32)
            acc_ref[h] = alpha[h] * acc_ref[h] + pv

    update(kth_ref, vth_ref, bh_ref, sh_ref, ph_ref, None)

    @pl.when(j == pl.num_programs(1) - 1)
    def _():
        row = lax.broadcasted_iota(jnp.int32, (t, t), 0)
        col = lax.broadcasted_iota(jnp.int32, (t, t), 1)
        update(ktn_ref, vtn_ref, bn_ref, sn_ref, pn_ref, col <= row)
        o_ref[0] = jnp.concatenate(
            [acc_ref[h][:, 0:HEAD_DIM] / acc_ref[h][:, HEAD_DIM:HEAD_DIM + 1] for h in range(N_HEADS)],
            axis=1).astype(BF16)


def _attn_sample(q, kt_new, vt_new, bias_new, kt_hist, vt_hist, bias_hist, tk):
    b, t, d = q.shape
    p = kt_hist.shape[-1]
    new = pl.BlockSpec((1, N_HEADS, HEAD_DIM, t), lambda bi, j: (bi, 0, 0, 0))
    hist = pl.BlockSpec((1, N_HEADS, HEAD_DIM, tk), lambda bi, j: (bi, 0, 0, j))
    rows = pl.BlockSpec((1, t, d), lambda bi, j: (bi, 0, 0))
    return pl.pallas_call(
        _attn_sample_kernel,
        grid=(b, p // tk),
        in_specs=[rows, hist, hist, pl.BlockSpec((1, N_HEADS, tk), lambda bi, j: (bi, 0, j)),
                  new, new, pl.BlockSpec((1, N_HEADS, t), lambda bi, j: (bi, 0, 0))],
        out_specs=rows,
        out_shape=jax.ShapeDtypeStruct((b, t, d), BF16),
        scratch_shapes=[pltpu.VMEM((N_HEADS, t, HEAD_DIM), BF16),
                        pltpu.VMEM((N_HEADS, t, tk), F32), pltpu.VMEM((N_HEADS, t, tk), BF16),
                        pltpu.VMEM((N_HEADS, t, t), F32), pltpu.VMEM((N_HEADS, t, t), BF16),
                        pltpu.VMEM((N_HEADS, t, 1), F32), pltpu.VMEM((N_HEADS, t, 2 * HEAD_DIM), F32)],
        compiler_params=_params(("arbitrary", "arbitrary")),
        name="attn_sample",
    )(q, kt_hist, vt_hist, bias_hist, kt_new, vt_new, bias_new)


def _merge_kernel(x_ref, g0_ref, b0_ref, cv_ref, o_ref, gc_ref, ga_ref, wao_ref, wo_ref, g1_ref, b1_ref,
                  y_ref, *, alpha):
    rc = x_ref.shape[0] // ROW_CHUNKS
    for c in range(ROW_CHUNKS):
        rows = slice(c * rc, (c + 1) * rc)
        xn = _layer_norm(x_ref[rows, :], g0_ref[...], b0_ref[...])
        attn_out = jnp.dot(o_ref[rows, :], wao_ref[...], preferred_element_type=F32)
        mix = (gc_ref[rows, :].astype(F32) * cv_ref[rows, :].astype(F32)
               + ga_ref[rows, :].astype(F32) * attn_out)
        y = jnp.dot(mix.astype(BF16), wo_ref[...], preferred_element_type=F32)
        y_ref[rows, :] = _layer_norm(alpha * xn + y, g1_ref[...], b1_ref[...])


def _merge(x, ln0_g, ln0_b, conv_out, attn, gate_c, gate_a, w_attn_out, w_o, ln1_g, ln1_b, alpha, tm):
    n, d = x.shape
    row = pl.BlockSpec((tm, d), lambda i: (i, 0))
    vec = _resident((1, d))
    return pl.pallas_call(
        functools.partial(_merge_kernel, alpha=alpha),
        grid=(n // tm,),
        in_specs=[row, vec, vec, row, row, row, row, _resident((d, d)), _resident((d, d)), vec, vec],
        out_specs=row,
        out_shape=jax.ShapeDtypeStruct((n, d), F32),
        compiler_params=_params(("arbitrary",)),
        name="merge",
    )(x, ln0_g, ln0_b, conv_out, attn, gate_c, gate_a, w_attn_out, w_o, ln1_g, ln1_b)


def _ffn_kernel(x_ref, p_ref, hist_ref, wua_ref, wub_ref, dw_ref, db_ref, wd_ref, wpg_ref, wp_ref, g_ref, b_ref,
                y_ref, tail_ref, ap_ref, *, tt, alpha):
    t = pl.program_id(1)

    @pl.when(t == 0)
    def _():
        ap_ref[0:FFN_HALO, :] = hist_ref[0]

    @pl.when(t > 0)
    def _():
        ap_ref[0:FFN_HALO, :] = ap_ref[tt:tt + FFN_HALO, :]

    x = x_ref[0]
    xb = x.astype(BF16)
    ap_ref[FFN_HALO:FFN_HALO + tt, :] = jnp.dot(xb, wua_ref[...], preferred_element_type=F32)
    tail_ref[0] = ap_ref[tt:tt + FFN_HALO, :]
    first = FFN_HALO - (FFN_K - 1)
    ac = db_ref[...]
    for k in range(FFN_K):
        ac = ac + dw_ref[k:k + 1, :] * ap_ref[first + k:first + k + tt, :]
    gate = jnp.dot(xb, wub_ref[...], preferred_element_type=F32)
    f = jnp.dot((jax.nn.silu(ac) * gate).astype(BF16), wd_ref[...], preferred_element_type=F32)
    ple = (jax.nn.sigmoid(jnp.dot(xb, wpg_ref[...], preferred_element_type=F32))
           * jnp.dot(p_ref[0].astype(BF16), wp_ref[...], preferred_element_type=F32))
    y_ref[0] = _layer_norm(alpha * x + f + ple, g_ref[...], b_ref[...])


def _ffn(x, p, hist, w_up_a, w_up_b, dw_w, dw_b, w_down, w_pg, w_ple, ln_g, ln_b, alpha, tt):
    b, t, d = x.shape
    dff = w_up_a.shape[1]
    dp = p.shape[-1]
    hist_pad = jnp.pad(hist.astype(F32), ((0, 0), (FFN_HALO - (FFN_K - 1), 0), (0, 0)))
    tile = pl.BlockSpec((1, tt, d), lambda i, j: (i, j, 0))
    halo = pl.BlockSpec((1, FFN_HALO, dff), lambda i, j: (i, 0, 0))
    return pl.pallas_call(
        functools.partial(_ffn_kernel, tt=tt, alpha=alpha),
        grid=(b, t // tt),
        in_specs=[tile, pl.BlockSpec((1, tt, dp), lambda i, j: (i, j, 0)), halo,
                  _resident((d, dff)), _resident((d, dff)), _resident((FFN_K, dff)), _resident((1, dff)),
                  _resident((dff, d)), _resident((d, d)), _resident((dp, d)), _resident((1, d)), _resident((1, d))],
        out_specs=(tile, halo),
        out_shape=(jax.ShapeDtypeStruct((b, t, d), F32), jax.ShapeDtypeStruct((b, FFN_HALO, dff), F32)),
        scratch_shapes=[pltpu.VMEM((FFN_HALO + tt, dff), F32)],
        compiler_params=_params(("arbitrary", "arbitrary")),
        name="ffn",
    )(x, p, hist_pad, w_up_a, w_up_b, dw_w, dw_b, w_down, w_pg, w_ple, ln_g, ln_b)


def _tile(n, target):
    if n <= target:
        return n
    for cand in range(target, SUBLANES - 1, -SUBLANES):
        if n % cand == 0:
            return cand
    return n


def _layer(x, p, conv_hist, ffn_hist, cache, w, alpha):
    b, t, d = x.shape
    n = b * t
    assert t >= CONV_K - 1 and t % SUBLANES == 0
    tm = _tile(n, 512)
    tt = _tile(t, 256)
    prompt = cache is None
    if prompt:
        tile = _tile(t, 512)
        assert tile % LANES == 0 and tm % tile == 0 and t % tm == 0
    u, q, kt, vt, logf_t, gate_c, gate_a, *copies = _inproj(x.reshape(n, d), w["ln_in_g"], w["ln_in_b"], w["in"],
                                                            w["b_f"], tm, t, bf16_copies=prompt)
    u = u.reshape(b, t, d)
    conv_out = _convbranch(u, conv_hist, w["conv_dw_w"], w["conv_dw_b"], w["conv_ln_g"], w["conv_ln_b"],
                           w["conv_out"], _tile(t, 512))
    q3 = q.reshape(b, t, d)
    per_head = lambda a: a.reshape(b, N_HEADS, HEAD_DIM, a.shape[-1])
    if prompt:
        kb, vtb = copies
        _, pieces = _decay_bias(logf_t)
        lanes = jnp.transpose(pieces, (0, 3, 2, 1)).reshape(b, t, N_HEADS * BIAS_PIECES)
        bias = jnp.pad(lanes, ((0, 0), (0, 0), (0, LANES - N_HEADS * BIAS_PIECES)))
        attn = _attn_prompt(q3, kb.reshape(b, t, d), bias, vtb, tile)
    else:
        k_hist, v_hist, logf_hist = cache
        past = k_hist.shape[1]
        time_last = lambda a: jnp.transpose(a.astype(F32), (0, 2, 3, 1))
        bias, _ = _decay_bias(jnp.concatenate([jnp.swapaxes(logf_hist.astype(F32), 1, 2), logf_t], axis=2))
        attn = _attn_sample(q3, per_head(kt), per_head(vt), bias[:, :, past:], time_last(k_hist), time_last(v_hist),
                            bias[:, :, :past], _tile(past, 1024))
    x1 = _merge(x.reshape(n, d), w["ln_in_g"], w["ln_in_b"], conv_out.reshape(n, d), attn.reshape(n, d),
                gate_c, gate_a, w["attn_out"], w["o"], w["ln1_g"], w["ln1_b"], alpha, tm)
    y, tail = _ffn(x1.reshape(b, t, d), p, ffn_hist, w["ffn_up_a"], w["ffn_up_b"], w["ffn_dw_w"], w["ffn_dw_b"],
                   w["ffn_down"], w["ple_gate"], w["ple"], w["ln2_g"], w["ln2_b"], alpha, tt)
    conv_new = u[:, t - (CONV_K - 1):]
    ffn_new = tail[:, FFN_HALO - (FFN_K - 1):]
    frames_first = lambda a: jnp.transpose(per_head(a), (0, 3, 1, 2))
    return (y, frames_first(kt), frames_first(vt), jnp.swapaxes(logf_t, 1, 2), conv_new, ffn_new)


def kernel(x_prompt, x_sample, cache_k, cache_v, cache_logf, state_conv, state_ffn_conv, p_prompt, p_sample, ln0_g, ln0_b, w_in, b_f, conv_dw_w, conv_dw_b, conv_ln_g, conv_ln_b, w_conv_out, w_attn_out, w_o, ln1_g, ln1_b, w_ffn_up, ffn_dw_w, ffn_dw_b, w_ffn_down, ln2_g, ln2_b, w_ple, w_ple_gate):
    depth = w_in.shape[0]
    assert depth == 1, "LN0 is fused into the first layer's projections; deeper stacks need an identity LN for later layers"
    alpha = (2.0 * depth) ** 0.25
    d = x_prompt.shape[-1]
    d_attn = N_HEADS * HEAD_DIM
    d_ff = w_ffn_up.shape[-1] // 2
    row = lambda a: a.reshape(1, -1).astype(F32)
    i = 0
    wi = w_in[i].astype(BF16)
    bounds = [0, d, 2 * d, 2 * d + d_attn, 2 * d + 2 * d_attn, 2 * d + 3 * d_attn, 2 * d + 3 * d_attn + N_HEADS,
              2 * d + 3 * d_attn + N_HEADS + d, 2 * d + 3 * d_attn + N_HEADS + 2 * d]
    names = ("a", "g", "q", "k", "v", "f", "gc", "ga")
    wup = w_ffn_up[i].astype(BF16)
    w_in_parts = {nm: wi[:, bounds[j]:bounds[j + 1]] for j, nm in enumerate(names)}
    w_in_parts["f"] = jnp.pad(w_in_parts["f"], ((0, 0), (0, LANES - N_HEADS)))
    w = {
        "ln_in_g": row(ln0_g), "ln_in_b": row(ln0_b),
        "in": w_in_parts,
        "b_f": b_f[i].reshape(-1, 1).astype(F32),
        "conv_dw_w": conv_dw_w[i].astype(F32), "conv_dw_b": row(conv_dw_b[i]),
        "conv_ln_g": row(conv_ln_g[i]), "conv_ln_b": row(conv_ln_b[i]),
        "conv_out": w_conv_out[i].astype(BF16), "attn_out": w_attn_out[i].astype(BF16), "o": w_o[i].astype(BF16),
        "ln1_g": row(ln1_g[i]), "ln1_b": row(ln1_b[i]),
        "ffn_up_a": wup[:, :d_ff], "ffn_up_b": wup[:, d_ff:],
        "ffn_dw_w": ffn_dw_w[i].astype(F32), "ffn_dw_b": row(ffn_dw_b[i]),
        "ffn_down": w_ffn_down[i].astype(BF16), "ple_gate": w_ple_gate[i].astype(BF16), "ple": w_ple[i].astype(BF16),
        "ln2_g": row(ln2_g[i]), "ln2_b": row(ln2_b[i]),
    }
    bp = x_prompt.shape[0]
    zeros_conv = jnp.zeros((bp, CONV_K - 1, d), F32)
    zeros_ffn = jnp.zeros((bp, FFN_K - 1, d_ff), F32)
    yp, kp, vp, lp, cp, fp = _layer(x_prompt, p_prompt[i], zeros_conv, zeros_ffn, None, w, alpha)
    ys, ks, vs, ls, cs, fs = _layer(x_sample, p_sample[i], state_conv[i], state_ffn_conv[i],
                                    (cache_k[i], cache_v[i], cache_logf[i]), w, alpha)
    stack = lambda a: a[None]
    return (yp, ys, stack(kp), stack(vp), stack(lp), stack(cp), stack(fp),
            stack(ks), stack(vs), stack(ls), stack(cs), stack(fs))
```

```python
import functools

import jax
import jax.numpy as jnp
from jax import lax
from jax.experimental import pallas as pl
from jax.experimental.pallas import tpu as pltpu

N_HEADS = 16
HEAD_DIM = 64
CONV_K = 31
FFN_K = 3
LN_EPS = 1e-5

LANES = 128
SUBLANES = 8
HEADS_PER_TILE = LANES // HEAD_DIM
CONV_HALO = 32
FFN_HALO = SUBLANES
MASK_VALUE = -1e30
LOG2E = 1.4426950408889634
Q_SCALE = LOG2E * HEAD_DIM ** -0.5
BIAS_PIECES = 3
ONES_ROWS = 16
LANE_TILES_PER_STEP = 4
ROW_CHUNKS = 2
PIPELINE_UNROLL = 4
VMEM_LIMIT = 56 * 1024 * 1024

F32 = jnp.float32
BF16 = jnp.bfloat16


def _layer_norm(x, g, b):
    mu = jnp.mean(x, axis=-1, keepdims=True)
    xc = x - mu
    var = jnp.mean(xc * xc, axis=-1, keepdims=True)
    return xc * lax.rsqrt(var + LN_EPS) * g + b


def _resident(shape):
    return pl.BlockSpec(shape, lambda *_: (0,) * len(shape), pipeline_mode=pl.Buffered(1))


def _params(semantics):
    return pltpu.CompilerParams(dimension_semantics=semantics, vmem_limit_bytes=VMEM_LIMIT)


def _inproj_kernel(x_ref, g_ref, b_ref, wa_ref, wg_ref, wq_ref, wk_ref, wv_ref, wf_ref, wgc_ref,
                   wga_ref, bf_ref, u_ref, q_ref, kt_ref, vt_ref, lft_ref, gc_ref, ga_ref, *rest, ts):
    rc = x_ref.shape[0] // ROW_CHUNKS
    piece = min(ts, rc)
    for c in range(ROW_CHUNKS):
        rows = slice(c * rc, (c + 1) * rc)
        xn = _layer_norm(x_ref[rows, :], g_ref[...], b_ref[...]).astype(BF16)

        def proj(w_ref):
            return jnp.dot(xn, w_ref[...], preferred_element_type=F32)

        u_ref[rows, :] = proj(wa_ref) * jax.nn.sigmoid(proj(wg_ref))
        q_ref[rows, :] = (proj(wq_ref) * Q_SCALE).astype(BF16)
        k = proj(wk_ref)
        kt = jnp.transpose(k)
        vt = jnp.transpose(proj(wv_ref))
        logits_t = jnp.transpose(proj(wf_ref))[0:N_HEADS, :]
        lft = jax.nn.log_sigmoid(logits_t + bf_ref[...])
        for s in range(rc // piece):
            src = slice(s * piece, (s + 1) * piece)
            first = c * rc + s * piece
            dst = (first // ts, slice(None), slice(first % ts, first % ts + piece))
            kt_ref[dst] = kt[:, src]
            vt_ref[dst] = vt[:, src]
            lft_ref[dst] = lft[:, src]
        if rest:
            kb_ref, vtb_ref = rest
            kb_ref[rows, :] = k.astype(BF16)
            vtb_ref[0, :, rows] = vt.astype(BF16)
        gc_ref[rows, :] = jax.nn.sigmoid(proj(wgc_ref)).astype(BF16)
        ga_ref[rows, :] = jax.nn.sigmoid(proj(wga_ref)).astype(BF16)


def _inproj(x, ln_g, ln_b, w, b_f, tm, t, bf16_copies):
    n, d = x.shape
    b = n // t
    ts = min(tm, t)
    row = lambda width: pl.BlockSpec((tm, width), lambda i: (i, 0))
    if tm <= t:
        tiles = t // tm
        tspec = lambda r: pl.BlockSpec((1, r, tm), lambda i: (i // tiles, 0, i % tiles))
    else:
        tspec = lambda r: pl.BlockSpec((tm // t, r, t), lambda i: (i, 0, 0))
    out_shape = [jax.ShapeDtypeStruct((n, d), F32), jax.ShapeDtypeStruct((n, d), BF16),
                 jax.ShapeDtypeStruct((b, d, t), F32), jax.ShapeDtypeStruct((b, d, t), F32),
                 jax.ShapeDtypeStruct((b, N_HEADS, t), F32),
                 jax.ShapeDtypeStruct((n, d), BF16), jax.ShapeDtypeStruct((n, d), BF16)]
    out_specs = [row(d), row(d), tspec(d), tspec(d), tspec(N_HEADS), row(d), row(d)]
    if bf16_copies:
        assert tm <= t
        out_shape += [jax.ShapeDtypeStruct((n, d), BF16), jax.ShapeDtypeStruct((b, d, t), BF16)]
        out_specs += [row(d), tspec(d)]
    return pl.pallas_call(
        functools.partial(_inproj_kernel, ts=ts),
        grid=(n // tm,),
        in_specs=[row(d), _resident((1, d)), _resident((1, d))]
                 + [_resident(w[name].shape) for name in ("a", "g", "q", "k", "v", "f", "gc", "ga")]
                 + [_resident((N_HEADS, 1))],
        out_specs=tuple(out_specs),
        out_shape=tuple(out_shape),
        compiler_params=_params(("arbitrary",)),
        name="inproj",
    )(x, ln_g, ln_b, w["a"], w["g"], w["q"], w["k"], w["v"], w["f"], w["gc"], w["ga"], b_f)


def _cumsum_kernel(x_ref, o_ref, p_ref):
    x = x_ref[0]
    nblk = x.shape[1]
    lane = lax.broadcasted_iota(jnp.int32, x.shape, 2)
    shift = 1
    while shift < LANES:
        x = x + jnp.where(lane >= shift, pltpu.roll(x, shift, 2), 0.0)
        shift *= 2
    tot = jnp.broadcast_to(x[:, :, LANES - 1:LANES], x.shape)
    blk = lax.broadcasted_iota(jnp.int32, x.shape, 1)
    inc = tot
    shift = 1
    while shift < nblk:
        inc = inc + jnp.where(blk >= shift, pltpu.roll(inc, shift, 1), 0.0)
        shift *= 2
    bias = (x + (inc - tot)) * (-LOG2E)
    o_ref[0] = bias
    rest = bias
    for piece in range(BIAS_PIECES):
        part = rest.astype(BF16)
        p_ref[0, piece] = part
        rest = rest - part.astype(F32)


def _decay_bias(logf_t):
    b, h, l = logf_t.shape
    nblk = pl.cdiv(l, LANES)
    x = jnp.pad(logf_t, ((0, 0), (0, 0), (0, nblk * LANES - l))).reshape(b, h, nblk, LANES)
    spec = pl.BlockSpec((1, h, nblk, LANES), lambda i: (i, 0, 0, 0))
    bias, pieces = pl.pallas_call(
        _cumsum_kernel, grid=(b,), in_specs=[spec],
        out_specs=(spec, pl.BlockSpec((1, BIAS_PIECES, h, nblk, LANES), lambda i: (i, 0, 0, 0, 0))),
        out_shape=(jax.ShapeDtypeStruct(x.shape, F32),
                   jax.ShapeDtypeStruct((b, BIAS_PIECES, h, nblk, LANES), BF16)),
        compiler_params=_params(("arbitrary",)), name="cumsum",
    )(x)
    return bias.reshape(b, h, nblk * LANES)[:, :, :l], pieces.reshape(b, BIAS_PIECES, h, nblk * LANES)[:, :, :, :l]


def _convbranch_kernel(u_ref, hist_ref, w_ref, b_ref, g_ref, beta_ref, wo_ref, o_ref, xp_ref, uc_ref, xs_ref,
                       *, tt, rows):
    t = pl.program_id(1)

    @pl.when(t == 0)
    def _():
        xp_ref[0:CONV_HALO, :] = hist_ref[0]

    @pl.when(t > 0)
    def _():
        xp_ref[0:CONV_HALO, :] = xp_ref[tt:tt + CONV_HALO, :]

    xp_ref[CONV_HALO:CONV_HALO + tt, :] = u_ref[0]
    first = CONV_HALO - (CONV_K - 1)
    d = u_ref.shape[-1]

    span = CONV_HALO - SUBLANES + tt

    def lane_block(c, carry):
        c0 = pl.multiple_of(c * LANES, LANES)
        cols = pl.ds(c0, LANES)
        for r in range(1, SUBLANES):
            xs_ref[r, 0:span, :] = xp_ref[r:r + span, cols]
        for r0 in range(0, tt, rows):
            acc = jnp.broadcast_to(b_ref[:, cols], (rows, LANES))
            for k in range(CONV_K):
                shift = (first + k) % SUBLANES
                base = r0 + first + k - shift
                if shift == 0:
                    x = xp_ref[base:base + rows, cols]
                else:
                    x = xs_ref[shift, base:base + rows, :]
                acc = acc + w_ref[k:k + 1, cols] * x
            uc_ref[r0:r0 + rows, cols] = acc
        return carry

    lax.fori_loop(0, d // LANES, lane_block, 0)
    act = jax.nn.silu(_layer_norm(uc_ref[...], g_ref[...], beta_ref[...])).astype(BF16)
    o_ref[0] = jnp.dot(act, wo_ref[...], preferred_element_type=F32).astype(BF16)


def _convbranch(u, hist, dw_w, dw_b, ln_g, ln_b, w_out, tt):
    b, t, d = u.shape
    hist_pad = jnp.pad(hist.astype(F32), ((0, 0), (CONV_HALO - (CONV_K - 1), 0), (0, 0)))
    tile = pl.BlockSpec((1, tt, d), lambda i, j: (i, j, 0))
    return pl.pallas_call(
        functools.partial(_convbranch_kernel, tt=tt, rows=min(tt, 64)),
        grid=(b, t // tt),
        in_specs=[tile, pl.BlockSpec((1, CONV_HALO, d), lambda i, j: (i, 0, 0)),
                  _resident((CONV_K, d)), _resident((1, d)), _resident((1, d)), _resident((1, d)),
                  _resident((d, d))],
        out_specs=tile,
        out_shape=jax.ShapeDtypeStruct((b, t, d), BF16),
        scratch_shapes=[pltpu.VMEM((CONV_HALO + tt, d), F32), pltpu.VMEM((tt, d), F32),
                        pltpu.VMEM((SUBLANES, CONV_HALO - SUBLANES + tt, LANES), F32)],
        compiler_params=_params(("arbitrary", "arbitrary")),
        name="convbranch",
    )(u, hist_pad, dw_w, dw_b, ln_g, ln_b, w_out)


def _attn_prompt_kernel(q_ref, k_ref, bias_ref, vt_ref, o_ref, qa_ref, m_ref, acc_ref, sa_ref, sb_ref, ra_ref, rb_ref,
                        *, tile):
    g = pl.program_id(1)
    i = pl.program_id(2)
    s_refs = (sa_ref, sb_ref)
    r_refs = (ra_ref, rb_ref)
    lane = lax.broadcasted_iota(jnp.int32, (tile, LANES), 1)
    heads = LANE_TILES_PER_STEP * HEADS_PER_TILE
    width = HEADS_PER_TILE * tile
    for lt in range(LANE_TILES_PER_STEP):
        q = q_ref[0, :, lt * LANES:(lt + 1) * LANES]
        for hh in range(HEADS_PER_TILE):
            h = lt * HEADS_PER_TILE + hh
            head = g * heads + h
            rows = slice(h * tile, (h + 1) * tile)
            qa_ref[rows, 0:LANES] = jnp.where((lane >= hh * HEAD_DIM) & (lane < (hh + 1) * HEAD_DIM), q,
                                              jnp.zeros_like(q))
            qa_ref[rows, LANES:2 * LANES] = jnp.where(
                (lane >= BIAS_PIECES * head) & (lane < BIAS_PIECES * (head + 1)), 1.0, 0.0).astype(BF16)
    m_ref[...] = jnp.full(m_ref.shape, MASK_VALUE, F32)
    acc_ref[...] = jnp.zeros(acc_ref.shape, F32)
    ones = jnp.ones((ONES_ROWS, tile), BF16)

    def produce(j, slot, mask):
        k0 = pl.multiple_of(j * tile, tile)
        bias = bias_ref[0, pl.ds(k0, tile), :]
        for lt in range(LANE_TILES_PER_STEP):
            cols = slice(lt * width, (lt + 1) * width)
            kb = jnp.concatenate([k_ref[0, pl.ds(k0, tile), lt * LANES:(lt + 1) * LANES], bias], axis=1)
            s = lax.dot_general(kb, qa_ref[cols, :], (((1,), (1,)), ((), ())), preferred_element_type=F32)
            if mask is not None:
                s = jnp.concatenate([jnp.where(mask, s[:, h * tile:(h + 1) * tile], MASK_VALUE)
                                     for h in range(HEADS_PER_TILE)], axis=1)
            s_refs[slot][:, cols] = s
            r_refs[slot][:, cols] = jnp.max(s, axis=0, keepdims=True)

    def consume(j, slot):
        k0 = pl.multiple_of(j * tile, tile)
        m_old = m_ref[...]
        m_new = jnp.maximum(m_old, r_refs[slot][...])
        alpha = jnp.exp2(m_old - m_new)
        m_ref[...] = m_new
        p = jnp.exp2(s_refs[slot][...] - m_new).astype(BF16)
        for h in range(heads):
            cols = slice(h * tile, (h + 1) * tile)
            vt1 = jnp.concatenate([vt_ref[0, h * HEAD_DIM:(h + 1) * HEAD_DIM, pl.ds(k0, tile)], ones], axis=0)
            acc_ref[h] = alpha[:, cols] * acc_ref[h] + jnp.dot(vt1, p[:, cols], preferred_element_type=F32)

    n_full = i
    key = lax.broadcasted_iota(jnp.int32, (tile, tile), 0)
    query = lax.broadcasted_iota(jnp.int32, (tile, tile), 1)
    produce(n_full, 0, key <= query)
    held = lambda t: jnp.where(t == 0, n_full, t - 1)

    def steps(t, count):
        for c in range(count):
            produce(t + c, (c + 1) % 2, None)
            consume(held(t + c), c % 2)

    def unrolled_steps(u, carry):
        steps(PIPELINE_UNROLL * u, PIPELINE_UNROLL)
        return carry

    lax.fori_loop(0, n_full // PIPELINE_UNROLL, unrolled_steps, 0)

    @pl.when(n_full % PIPELINE_UNROLL >= 2)
    def _():
        steps(PIPELINE_UNROLL * (n_full // PIPELINE_UNROLL), 2)

    t0 = 2 * (n_full // 2)

    @pl.when(n_full % 2 == 1)
    def _():
        steps(t0, 1)
        consume(held(t0 + 1), 1)

    @pl.when(n_full % 2 == 0)
    def _():
        consume(held(t0), 0)

    out = [acc_ref[h][0:HEAD_DIM, :] / acc_ref[h][HEAD_DIM:HEAD_DIM + 1, :] for h in range(heads)]
    o_ref[0] = jnp.transpose(jnp.concatenate(out, axis=0)).astype(BF16)


def _attn_prompt(q, k, bias, vt, tile):
    b, t, d = q.shape
    w = LANE_TILES_PER_STEP * LANES
    heads = LANE_TILES_PER_STEP * HEADS_PER_TILE
    assert d % w == 0
    return pl.pallas_call(
        functools.partial(_attn_prompt_kernel, tile=tile),
        grid=(b, d // w, t // tile),
        in_specs=[pl.BlockSpec((1, tile, w), lambda bi, g, i: (bi, i, g)),
                  pl.BlockSpec((1, t, w), lambda bi, g, i: (bi, 0, g), pipeline_mode=pl.Buffered(1)),
                  pl.BlockSpec((1, t, LANES), lambda bi, g, i: (bi, 0, 0), pipeline_mode=pl.Buffered(1)),
                  pl.BlockSpec((1, w, t), lambda bi, g, i: (bi, g, 0), pipeline_mode=pl.Buffered(1))],
        out_specs=pl.BlockSpec((1, tile, w), lambda bi, g, i: (bi, i, g)),
        out_shape=jax.ShapeDtypeStruct((b, t, d), BF16),
        scratch_shapes=[pltpu.VMEM((heads * tile, 2 * LANES), BF16),
                        pltpu.VMEM((1, heads * tile), F32),
                        pltpu.VMEM((heads, HEAD_DIM + ONES_ROWS, tile), F32),
                        pltpu.VMEM((tile, heads * tile), F32), pltpu.VMEM((tile, heads * tile), F32),
                        pltpu.VMEM((1, heads * tile), F32), pltpu.VMEM((1, heads * tile), F32)],
        compiler_params=_params(("arbitrary", "arbitrary", "arbitrary")),
        name="attn_prompt",
    )(q, k, bias, vt)


def _attn_sample_kernel(q_ref, kth_ref, vth_ref, bh_ref, ktn_ref, vtn_ref, bn_ref, o_ref,
                        qh_ref, sh_ref, ph_ref, sn_ref, pn_ref, m_ref, acc_ref):
    j = pl.program_id(1)
    t = q_ref.shape[1]

    @pl.when(j == 0)
    def _():
        m_ref[...] = jnp.full(m_ref.shape, MASK_VALUE, F32)
        acc_ref[...] = jnp.zeros(acc_ref.shape, F32)
        for h in range(N_HEADS):
            qh_ref[h] = q_ref[0, :, h * HEAD_DIM:(h + 1) * HEAD_DIM]

    def update(kt_ref, vt_ref, bias_ref, s_ref, p_ref, mask):
        keys = kt_ref.shape[-1]
        for h in range(N_HEADS):
            s = jnp.dot(qh_ref[h], kt_ref[0, h].astype(BF16), preferred_element_type=F32) + bias_ref[0, h:h + 1, :]
            s_ref[h] = s if mask is None else jnp.where(mask, s, MASK_VALUE)
        s = s_ref[...]
        m_old = m_ref[...]
        m_new = jnp.maximum(m_old, jnp.max(s, axis=2, keepdims=True))
        alpha = jnp.exp2(m_old - m_new)
        m_ref[...] = m_new
        p_ref[...] = jnp.exp2(s - m_new).astype(BF16)
        ones = jnp.ones((HEAD_DIM, keys), BF16)
        for h in range(N_HEADS):
            vt1 = jnp.concatenate([vt_ref[0, h].astype(BF16), ones], axis=0)
            pv = lax.dot_general(p_ref[h], vt1, (((1,), (1,)), ((), ())), preferred_element_type=F32)
            acc_ref[h] = alpha[h] * acc_ref[h] + pv

    update(kth_ref, vth_ref, bh_ref, sh_ref, ph_ref, None)

    @pl.when(j == pl.num_programs(1) - 1)
    def _():
        row = lax.broadcasted_iota(jnp.int32, (t, t), 0)
        col = lax.broadcasted_iota(jnp.int32, (t, t), 1)
        update(ktn_ref, vtn_ref, bn_ref, sn_ref, pn_ref, col <= row)
        o_ref[0] = jnp.concatenate(
            [acc_ref[h][:, 0:HEAD_DIM] / acc_ref[h][:, HEAD_DIM:HEAD_DIM + 1] for h in range(N_HEADS)],
            axis=1).astype(BF16)


def _attn_sample(q, kt_new, vt_new, bias_new, kt_hist, vt_hist, bias_hist, tk):
    b, t, d = q.shape
    p = kt_hist.shape[-1]
    new = pl.BlockSpec((1, N_HEADS, HEAD_DIM, t), lambda bi, j: (bi, 0, 0, 0))
    hist = pl.BlockSpec((1, N_HEADS, HEAD_DIM, tk), lambda bi, j: (bi, 0, 0, j))
    rows = pl.BlockSpec((1, t, d), lambda bi, j: (bi, 0, 0))
    return pl.pallas_call(
        _attn_sample_kernel,
        grid=(b, p // tk),
        in_specs=[rows, hist, hist, pl.BlockSpec((1, N_HEADS, tk), lambda bi, j: (bi, 0, j)),
                  new, new, pl.BlockSpec((1, N_HEADS, t), lambda bi, j: (bi, 0, 0))],
        out_specs=rows,
        out_shape=jax.ShapeDtypeStruct((b, t, d), BF16),
        scratch_shapes=[pltpu.VMEM((N_HEADS, t, HEAD_DIM), BF16),
                        pltpu.VMEM((N_HEADS, t, tk), F32), pltpu.VMEM((N_HEADS, t, tk), BF16),
                        pltpu.VMEM((N_HEADS, t, t), F32), pltpu.VMEM((N_HEADS, t, t), BF16),
                        pltpu.VMEM((N_HEADS, t, 1), F32), pltpu.VMEM((N_HEADS, t, 2 * HEAD_DIM), F32)],
        compiler_params=_params(("arbitrary", "arbitrary")),
        name="attn_sample",
    )(q, kt_hist, vt_hist, bias_hist, kt_new, vt_new, bias_new)


def _merge_kernel(x_ref, g0_ref, b0_ref, cv_ref, o_ref, gc_ref, ga_ref, wao_ref, wo_ref, g1_ref, b1_ref,
                  y_ref, *, alpha):
    rc = x_ref.shape[0] // ROW_CHUNKS
    for c in range(ROW_CHUNKS):
        rows = slice(c * rc, (c + 1) * rc)
        xn = _layer_norm(x_ref[rows, :], g0_ref[...], b0_ref[...])
        attn_out = jnp.dot(o_ref[rows, :], wao_ref[...], preferred_element_type=F32)
        mix = (gc_ref[rows, :].astype(F32) * cv_ref[rows, :].astype(F32)
               + ga_ref[rows, :].astype(F32) * attn_out)
        y = jnp.dot(mix.astype(BF16), wo_ref[...], preferred_element_type=F32)
        y_ref[rows, :] = _layer_norm(alpha * xn + y, g1_ref[...], b1_ref[...])


def _merge(x, ln0_g, ln0_b, conv_out, attn, gate_c, gate_a, w_attn_out, w_o, ln1_g, ln1_b, alpha, tm):
    n, d = x.shape
    row = pl.BlockSpec((tm, d), lambda i: (i, 0))
    vec = _resident((1, d))
    return pl.pallas_call(
        functools.partial(_merge_kernel, alpha=alpha),
        grid=(n // tm,),
        in_specs=[row, vec, vec, row, row, row, row, _resident((d, d)), _resident((d, d)), vec, vec],
        out_specs=row,
        out_shape=jax.ShapeDtypeStruct((n, d), F32),
        compiler_params=_params(("arbitrary",)),
        name="merge",
    )(x, ln0_g, ln0_b, conv_out, attn, gate_c, gate_a, w_attn_out, w_o, ln1_g, ln1_b)


def _ffn_kernel(x_ref, p_ref, hist_ref, wua_ref, wub_ref, dw_ref, db_ref, wd_ref, wpg_ref, wp_ref, g_ref, b_ref,
                y_ref, tail_ref, ap_ref, *, tt, alpha):
    t = pl.program_id(1)

    @pl.when(t == 0)
    def _():
        ap_ref[0:FFN_HALO, :] = hist_ref[0]

    @pl.when(t > 0)
    def _():
        ap_ref[0:FFN_HALO, :] = ap_ref[tt:tt + FFN_HALO, :]

    x = x_ref[0]
    xb = x.astype(BF16)
    ap_ref[FFN_HALO:FFN_HALO + tt, :] = jnp.dot(xb, wua_ref[...], preferred_element_type=F32)
    tail_ref[0] = ap_ref[tt:tt + FFN_HALO, :]
    first = FFN_HALO - (FFN_K - 1)
    ac = db_ref[...]
    for k in range(FFN_K):
        ac = ac + dw_ref[k:k + 1, :] * ap_ref[first + k:first + k + tt, :]
    gate = jnp.dot(xb, wub_ref[...], preferred_element_type=F32)
    f = jnp.dot((jax.nn.silu(ac) * gate).astype(BF16), wd_ref[...], preferred_element_type=F32)
    ple = (jax.nn.sigmoid(jnp.dot(xb, wpg_ref[...], preferred_element_type=F32))
           * jnp.dot(p_ref[0].astype(BF16), wp_ref[...], preferred_element_type=F32))
    y_ref[0] = _layer_norm(alpha * x + f + ple, g_ref[...], b_ref[...])


def _ffn(x, p, hist, w_up_a, w_up_b, dw_w, dw_b, w_down, w_pg, w_ple, ln_g, ln_b, alpha, tt):
    b, t, d = x.shape
    dff = w_up_a.shape[1]
    dp = p.shape[-1]
    hist_pad = jnp.pad(hist.astype(F32), ((0, 0), (FFN_HALO - (FFN_K - 1), 0), (0, 0)))
    tile = pl.BlockSpec((1, tt, d), lambda i, j: (i, j, 0))
    halo = pl.BlockSpec((1, FFN_HALO, dff), lambda i, j: (i, 0, 0))
    return pl.pallas_call(
        functools.partial(_ffn_kernel, tt=tt, alpha=alpha),
        grid=(b, t // tt),
        in_specs=[tile, pl.BlockSpec((1, tt, dp), lambda i, j: (i, j, 0)), halo,
                  _resident((d, dff)), _resident((d, dff)), _resident((FFN_K, dff)), _resident((1, dff)),
                  _resident((dff, d)), _resident((d, d)), _resident((dp, d)), _resident((1, d)), _resident((1, d))],
        out_specs=(tile, halo),
        out_shape=(jax.ShapeDtypeStruct((b, t, d), F32), jax.ShapeDtypeStruct((b, FFN_HALO, dff), F32)),
        scratch_shapes=[pltpu.VMEM((FFN_HALO + tt, dff), F32)],
        compiler_params=_params(("arbitrary", "arbitrary")),
        name="ffn",
    )(x, p, hist_pad, w_up_a, w_up_b, dw_w, dw_b, w_down, w_pg, w_ple, ln_g, ln_b)


def _tile(n, target):
    if n <= target:
        return n
    for cand in range(target, SUBLANES - 1, -SUBLANES):
        if n % cand == 0:
            return cand
    return n


def _layer(x, p, conv_hist, ffn_hist, cache, w, alpha):
    b, t, d = x.shape
    n = b * t
    assert t >= CONV_K - 1 and t % SUBLANES == 0
    tm = _tile(n, 512)
    tt = _tile(t, 256)
    prompt = cache is None
    if prompt:
        tile = _tile(t, 512)
        assert tile % LANES == 0 and tm % tile == 0 and t % tm == 0
    u, q, kt, vt, logf_t, gate_c, gate_a, *copies = _inproj(x.reshape(n, d), w["ln_in_g"], w["ln_in_b"], w["in"],
                                                            w["b_f"], tm, t, bf16_copies=prompt)
    u = u.reshape(b, t, d)
    conv_out = _convbranch(u, conv_hist, w["conv_dw_w"], w["conv_dw_b"], w["conv_ln_g"], w["conv_ln_b"],
                           w["conv_out"], _tile(t, 512))
    q3 = q.reshape(b, t, d)
    per_head = lambda a: a.reshape(b, N_HEADS, HEAD_DIM, a.shape[-1])
    if prompt:
        kb, vtb = copies
        _, pieces = _decay_bias(logf_t)
        lanes = jnp.transpose(pieces, (0, 3, 2, 1)).reshape(b, t, N_HEADS * BIAS_PIECES)
        bias = jnp.pad(lanes, ((0, 0), (0, 0), (0, LANES - N_HEADS * BIAS_PIECES)))
        attn = _attn_prompt(q3, kb.reshape(b, t, d), bias, vtb, tile)
    else:
        k_hist, v_hist, logf_hist = cache
        past = k_hist.shape[1]
        time_last = lambda a: jnp.transpose(a.astype(F32), (0, 2, 3, 1))
        bias, _ = _decay_bias(jnp.concatenate([jnp.swapaxes(logf_hist.astype(F32), 1, 2), logf_t], axis=2))
        attn = _attn_sample(q3, per_head(kt), per_head(vt), bias[:, :, past:], time_last(k_hist), time_last(v_hist),
                            bias[:, :, :past], _tile(past, 1024))
    x1 = _merge(x.reshape(n, d), w["ln_in_g"], w["ln_in_b"], conv_out.reshape(n, d), attn.reshape(n, d),
                gate_c, gate_a, w["attn_out"], w["o"], w["ln1_g"], w["ln1_b"], alpha, tm)
    y, tail = _ffn(x1.reshape(b, t, d), p, ffn_hist, w["ffn_up_a"], w["ffn_up_b"], w["ffn_dw_w"], w["ffn_dw_b"],
                   w["ffn_down"], w["ple_gate"], w["ple"], w["ln2_g"], w["ln2_b"], alpha, tt)
    conv_new = u[:, t - (CONV_K - 1):]
    ffn_new = tail[:, FFN_HALO - (FFN_K - 1):]
    frames_first = lambda a: jnp.transpose(per_head(a), (0, 3, 1, 2))
    return (y, frames_first(kt), frames_first(vt), jnp.swapaxes(logf_t, 1, 2), conv_new, ffn_new)


def kernel(x_prompt, x_sample, cache_k, cache_v, cache_logf, state_conv, state_ffn_conv, p_prompt, p_sample, ln0_g, ln0_b, w_in, b_f, conv_dw_w, conv_dw_b, conv_ln_g, conv_ln_b, w_conv_out, w_attn_out, w_o, ln1_g, ln1_b, w_ffn_up, ffn_dw_w, ffn_dw_b, w_ffn_down, ln2_g, ln2_b, w_ple, w_ple_gate):
    depth = w_in.shape[0]
    assert depth == 1, "LN0 is fused into the first layer's projections; deeper stacks need an identity LN for later layers"
    alpha = (2.0 * depth) ** 0.25
    d = x_prompt.shape[-1]
    d_attn = N_HEADS * HEAD_DIM
    d_ff = w_ffn_up.shape[-1] // 2
    row = lambda a: a.reshape(1, -1).astype(F32)
    i = 0
    wi = w_in[i].astype(BF16)
    bounds = [0, d, 2 * d, 2 * d + d_attn, 2 * d + 2 * d_attn, 2 * d + 3 * d_attn, 2 * d + 3 * d_attn + N_HEADS,
              2 * d + 3 * d_attn + N_HEADS + d, 2 * d + 3 * d_attn + N_HEADS + 2 * d]
    names = ("a", "g", "q", "k", "v", "f", "gc", "ga")
    wup = w_ffn_up[i].astype(BF16)
    w_in_parts = {nm: wi[:, bounds[j]:bounds[j + 1]] for j, nm in enumerate(names)}
    w_in_parts["f"] = jnp.pad(w_in_parts["f"], ((0, 0), (0, LANES - N_HEADS)))
    w = {
        "ln_in_g": row(ln0_g), "ln_in_b": row(ln0_b),
        "in": w_in_parts,
        "b_f": b_f[i].reshape(-1, 1).astype(F32),
        "conv_dw_w": conv_dw_w[i].astype(F32), "conv_dw_b": row(conv_dw_b[i]),
        "conv_ln_g": row(conv_ln_g[i]), "conv_ln_b": row(conv_ln_b[i]),
        "conv_out": w_conv_out[i].astype(BF16), "attn_out": w_attn_out[i].astype(BF16), "o": w_o[i].astype(BF16),
        "ln1_g": row(ln1_g[i]), "ln1_b": row(ln1_b[i]),
        "ffn_up_a": wup[:, :d_ff], "ffn_up_b": wup[:, d_ff:],
        "ffn_dw_w": ffn_dw_w[i].astype(F32), "ffn_dw_b": row(ffn_dw_b[i]),
        "ffn_down": w_ffn_down[i].astype(BF16), "ple_gate": w_ple_gate[i].astype(BF16), "ple": w_ple[i].astype(BF16),
        "ln2_g": row(ln2_g[i]), "ln2_b": row(ln2_b[i]),
    }
    bp = x_prompt.shape[0]
    zeros_conv = jnp.zeros((bp, CONV_K - 1, d), F32)
    zeros_ffn = jnp.zeros((bp, FFN_K - 1, d_ff), F32)
    yp, kp, vp, lp, cp, fp = _layer(x_prompt, p_prompt[i], zeros_conv, zeros_ffn, None, w, alpha)
    ys, ks, vs, ls, cs, fs = _layer(x_sample, p_sample[i], state_conv[i], state_ffn_conv[i],
                                    (cache_k[i], cache_v[i], cache_logf[i]), w, alpha)
    stack = lambda a: a[None]
    return (yp, ys, stack(kp), stack(vp), stack(lp), stack(cp), stack(fp),
            stack(ks), stack(vs), stack(ls), stack(cs), stack(fs))
```
